```python
import math
import jax, jax.numpy as jnp
from jax import lax
import numpy as np

D_MODEL = 4096
BATCH = 1
SEQ = 16384
DEPTH = 1
DEC_BATCH = 4
DEC_SEQ = 4096
PAST_LEN = 128

CONV_CHANNELS = D_MODEL // 4
CONV_WIDTH = 31
HEAD_DIM = 128
ATT_GROUPS = ((128, 1), (512, 4), (2048, 16))
N_ATT_GROUPS = len(ATT_GROUPS)
HEADS_PER_GROUP = (D_MODEL - CONV_CHANNELS) // (N_ATT_GROUPS * HEAD_DIM)
N_ATT_HEADS = HEADS_PER_GROUP * N_ATT_GROUPS
GROUP_WIDTH = HEADS_PER_GROUP * HEAD_DIM
IN_PROJ = 2 * CONV_CHANNELS + 3 * N_ATT_HEADS * HEAD_DIM
MIX_OUT = CONV_CHANNELS + GROUP_WIDTH
N_EXPERT_GROUPS = 4
EXPERTS_PER_GROUP = 8
N_EXPERTS = N_EXPERT_GROUPS * EXPERTS_PER_GROUP
TOP_K = 2
D_EXPERT = D_MODEL // 4
EXPERT_BLOCK = 128
LN_EPS = 1e-5
NEG_INF = -1e30
DEEPNORM_ALPHA = (2.0 * DEPTH) ** 0.25
DEEPNORM_BETA = (8.0 * DEPTH) ** -0.25

kernel_name = "hymba_conformer_longnet_hmoe_encoder"


def layer_norm(x, g, b):
    xf = x.astype(jnp.float32)
    mu = jnp.mean(xf, axis=-1, keepdims=True)
    xc = xf - mu
    var = jnp.mean(xc * xc, axis=-1, keepdims=True)
    return (xc * lax.rsqrt(var + LN_EPS) * g.astype(jnp.float32) + b.astype(jnp.float32)).astype(x.dtype)


def alibi_slopes(n):
    return jnp.exp2(-8.0 * jnp.arange(1, n + 1, dtype=jnp.float32) / n)


def depthwise_conv(u, w, b):
    C = u.shape[-1]
    pad = (CONV_WIDTH - 1) // 2
    y = lax.conv_general_dilated(u, w[:, None, :].astype(u.dtype), window_strides=(1,),
                                 padding=[(pad, pad)], dimension_numbers=('NWC', 'WIO', 'NWC'),
                                 feature_group_count=C)
    return y + b


def dilated_window_attention(q, k, v, slopes, window, dilation):
    B, S, H, Dh = q.shape
    R = dilation
    K = window // (2 * dilation)
    L = S // R
    nb = -(-L // K)
    Lp = nb * K

    def residues(t):
        t = t.reshape(B, L, R, H, Dh).transpose(0, 2, 1, 3, 4)
        return jnp.pad(t, ((0, 0), (0, 0), (0, Lp - L), (0, 0), (0, 0)))

    def neighbourhood(t):
        tp = jnp.pad(t, ((0, 0), (0, 0), (K, K), (0, 0), (0, 0)))
        return jnp.concatenate([tp[:, :, o:o + Lp].reshape(B, R, nb, K, H, Dh) for o in (0, K, 2 * K)], axis=3)

    qb = residues(q).reshape(B, R, nb, K, H, Dh)
    kb = neighbourhood(residues(k))
    vb = neighbourhood(residues(v))
    s = jnp.einsum('brnqhd,brnkhd->brnhqk', qb, kb, preferred_element_type=jnp.float32)
    blk = jnp.arange(nb)[:, None, None] * K
    qpos = blk + jnp.arange(K)[None, :, None]
    kpos = blk - K + jnp.arange(3 * K)[None, None, :]
    dist = jnp.abs(qpos - kpos)
    valid = (dist <= K) & (kpos >= 0) & (kpos < L)
    bias = -slopes.astype(jnp.float32)[None, :, None, None] * (R * dist).astype(jnp.float32)[:, None, :, :]
    s = jnp.where(valid[:, None], s + bias, NEG_INF)
    m = jnp.max(s, axis=-1, keepdims=True)
    p = jnp.exp(s - m)
    denom = jnp.sum(p, axis=-1, keepdims=True)
    o = jnp.einsum('brnhqk,brnkhd->brnqhd', (p / denom).astype(v.dtype), vb, preferred_element_type=jnp.float32)
    lse = (m + jnp.log(denom))[..., 0]
    o = o.reshape(B, R, Lp, H, Dh)[:, :, :L].transpose(0, 2, 1, 3, 4).reshape(B, S, H, Dh)
    lse = lse.transpose(0, 1, 2, 4, 3).reshape(B, R, Lp, H)[:, :, :L].transpose(0, 2, 1, 3).reshape(B, S, H)
    return o, lse


def routed_experts(xt, expert_idx, gates, w_gate, w_up, w_down):
    T, D = xt.shape
    A = T * TOP_K
    flat_e = expert_idx.reshape(A)
    flat_w = gates.reshape(A)
    flat_tok = jnp.repeat(jnp.arange(T, dtype=jnp.int32), TOP_K)
    order = jnp.argsort(flat_e)
    e_sorted = flat_e[order]
    counts = jnp.bincount(flat_e, length=N_EXPERTS)
    padded = (counts + EXPERT_BLOCK - 1) // EXPERT_BLOCK * EXPERT_BLOCK
    start = jnp.cumsum(counts) - counts
    pend = jnp.cumsum(padded)
    pstart = pend - padded
    dest = pstart[e_sorted] + jnp.arange(A) - start[e_sorted]
    n_blocks = -(-A // EXPERT_BLOCK) + N_EXPERTS
    n_rows = n_blocks * EXPERT_BLOCK
    buf_tok = jnp.zeros((n_rows,), jnp.int32).at[dest].set(flat_tok[order])
    buf_w = jnp.zeros((n_rows,), jnp.float32).at[dest].set(flat_w[order])
    block_e = jnp.minimum(jnp.searchsorted(pend, jnp.arange(n_blocks) * EXPERT_BLOCK, side='right'), N_EXPERTS - 1)

    def block_fn(args):
        tok, e = args
        xb = xt[tok]
        hb = jax.nn.silu(xb @ w_gate[e]) * (xb @ w_up[e])
        return hb @ w_down[e]

    yb = lax.map(block_fn, (buf_tok.reshape(n_blocks, EXPERT_BLOCK), block_e))
    yb = yb.reshape(n_rows, D).astype(jnp.float32) * buf_w[:, None]
    return jnp.zeros((T, D), jnp.float32).at[buf_tok].add(yb)


def hierarchical_moe(x, w_rg, b_rg, w_re, b_re, w_gate, w_up, w_down):
    B, S, D = x.shape
    T = B * S
    xt = x.reshape(T, D)
    group_logits = (xt @ w_rg + b_rg).astype(jnp.float32)
    group_probs = jax.nn.softmax(group_logits, axis=-1)
    g_star = jnp.argmax(group_logits, axis=-1)
    p_group = group_probs[jnp.arange(T), g_star][:, None]
    expert_logits = (xt @ w_re + b_re).astype(jnp.float32).reshape(T, N_EXPERT_GROUPS, EXPERTS_PER_GROUP)
    in_group = expert_logits[jnp.arange(T), g_star]
    top_v, top_i = lax.top_k(in_group, TOP_K)
    gates = jax.nn.softmax(top_v, axis=-1) * p_group
    expert_idx = (g_star[:, None] * EXPERTS_PER_GROUP + top_i).astype(jnp.int32)
    y = routed_experts(xt, expert_idx, gates, w_gate, w_up, w_down)
    return y.reshape(B, S, D).astype(x.dtype)


def encoder_trunk(x, ln_in_g, ln_in_b, w_in, b_in, conv_w, conv_b, conv_ln_g, conv_ln_b, w_out, b_out,
                  ln1_g, ln1_b, w_router_group, b_router_group, w_router_expert, b_router_expert,
                  w_gate, w_up, w_down, ln2_g, ln2_b):
    x = layer_norm(x, ln_in_g, ln_in_b)
    slopes = alibi_slopes(N_ATT_HEADS)
    C = CONV_CHANNELS
    for l in range(DEPTH):
        B, S, _ = x.shape
        h = jnp.einsum('bsd,dp->bsp', x, w_in[l]) + b_in[l]
        u = h[..., :C] * jax.nn.sigmoid(h[..., C:2 * C])
        u = depthwise_conv(u, conv_w[l], conv_b[l])
        u = jax.nn.silu(layer_norm(u, conv_ln_g[l], conv_ln_b[l]))
        outs, lses = [], []
        off = 2 * C
        for g, (window, dilation) in enumerate(ATT_GROUPS):
            q = h[..., off:off + GROUP_WIDTH].reshape(B, S, HEADS_PER_GROUP, HEAD_DIM) * (HEAD_DIM ** -0.5)
            k = h[..., off + GROUP_WIDTH:off + 2 * GROUP_WIDTH].reshape(B, S, HEADS_PER_GROUP, HEAD_DIM)
            v = h[..., off + 2 * GROUP_WIDTH:off + 3 * GROUP_WIDTH].reshape(B, S, HEADS_PER_GROUP, HEAD_DIM)
            off += 3 * GROUP_WIDTH
            o_g, lse_g = dilated_window_attention(q, k, v, slopes[g * HEADS_PER_GROUP:(g + 1) * HEADS_PER_GROUP],
                                                  window, dilation)
            outs.append(o_g)
            lses.append(lse_g)
        mix_w = jax.nn.softmax(jnp.stack(lses), axis=0)
        o = jnp.einsum('gbsh,gbshd->bshd', mix_w, jnp.stack(outs)).reshape(B, S, GROUP_WIDTH).astype(x.dtype)
        mix = jnp.einsum('bsc,cd->bsd', jnp.concatenate([u, o], axis=-1), w_out[l]) + b_out[l]
        x = layer_norm(DEEPNORM_ALPHA * x + mix, ln1_g[l], ln1_b[l])
        ffn = hierarchical_moe(x, w_router_group[l], b_router_group[l], w_router_expert[l], b_router_expert[l],
                               w_gate[l], w_up[l], w_down[l])
        x = layer_norm(DEEPNORM_ALPHA * x + ffn, ln2_g[l], ln2_b[l])
    return x


def setup_inputs(seed: int = 0) -> dict:
    key = jax.random.key(seed)
    ks = jax.random.split(key, 24)
    f32 = jnp.float32
    nrm = lambda k, shape, scale: jax.random.normal(k, shape, f32) * scale
    D = D_MODEL
    return {
        'x_prompt': jax.random.normal(ks[0], (BATCH, SEQ, D), f32),
        'x_sample': jax.random.normal(ks[1], (DEC_BATCH, DEC_SEQ, D), f32),
        'ln_in_g': 1.0 + nrm(ks[2], (D,), 0.02),
        'ln_in_b': nrm(ks[3], (D,), 0.02),
        'w_in': nrm(ks[4], (DEPTH, D, IN_PROJ), D ** -0.5),
        'b_in': nrm(ks[5], (DEPTH, IN_PROJ), 0.02),
        'conv_w': nrm(ks[6], (DEPTH, CONV_WIDTH, CONV_CHANNELS), CONV_WIDTH ** -0.5),
        'conv_b': nrm(ks[7], (DEPTH, CONV_CHANNELS), 0.02),
        'conv_ln_g': 1.0 + nrm(ks[8], (DEPTH, CONV_CHANNELS), 0.02),
        'conv_ln_b': nrm(ks[9], (DEPTH, CONV_CHANNELS), 0.02),
        'w_out': nrm(ks[10], (DEPTH, MIX_OUT, D), DEEPNORM_BETA * MIX_OUT ** -0.5),
        'b_out': nrm(ks[11], (DEPTH, D), 0.02),
        'ln1_g': 1.0 + nrm(ks[12], (DEPTH, D), 0.02),
        'ln1_b': nrm(ks[13], (DEPTH, D), 0.02),
        'w_router_group': nrm(ks[14], (DEPTH, D, N_EXPERT_GROUPS), D ** -0.5),
        'b_router_group': nrm(ks[15], (DEPTH, N_EXPERT_GROUPS), 0.01),
        'w_router_expert': nrm(ks[16], (DEPTH, D, N_EXPERTS), D ** -0.5),
        'b_router_expert': nrm(ks[17], (DEPTH, N_EXPERTS), 0.01),
        'w_gate': nrm(ks[18], (DEPTH, N_EXPERTS, D, D_EXPERT), D ** -0.5),
        'w_up': nrm(ks[19], (DEPTH, N_EXPERTS, D, D_EXPERT), D ** -0.5),
        'w_down': nrm(ks[20], (DEPTH, N_EXPERTS, D_EXPERT, D), DEEPNORM_BETA * D_EXPERT ** -0.5),
        'ln2_g': 1.0 + nrm(ks[21], (DEPTH, D), 0.02),
        'ln2_b': nrm(ks[22], (DEPTH, D), 0.02),
    }


def reference(x_prompt, x_sample, ln_in_g, ln_in_b, w_in, b_in, conv_w, conv_b, conv_ln_g, conv_ln_b,
              w_out, b_out, ln1_g, ln1_b, w_router_group, b_router_group, w_router_expert, b_router_expert,
              w_gate, w_up, w_down, ln2_g, ln2_b):
    y_prompt = encoder_trunk(x_prompt, ln_in_g, ln_in_b, w_in, b_in, conv_w, conv_b, conv_ln_g, conv_ln_b,
                             w_out, b_out, ln1_g, ln1_b, w_router_group, b_router_group, w_router_expert,
                             b_router_expert, w_gate, w_up, w_down, ln2_g, ln2_b)
    y_sample = encoder_trunk(x_sample, ln_in_g, ln_in_b, w_in, b_in, conv_w, conv_b, conv_ln_g, conv_ln_b,
                             w_out, b_out, ln1_g, ln1_b, w_router_group, b_router_group, w_router_expert,
                             b_router_expert, w_gate, w_up, w_down, ln2_g, ln2_b)
    return (y_prompt, y_sample)
```

```python
import functools

import jax
import jax.numpy as jnp
from jax import lax
from jax.experimental import pallas as pl
from jax.experimental.pallas import tpu as pltpu

F32 = jnp.float32
BF16 = jnp.bfloat16

HEAD_DIM = 128
ATT_GROUPS = ((128, 1), (512, 4), (2048, 16))
HALF_WINDOW = 64
CONV_WIDTH = 31
CONV_HALO = 16
TOP_K = 2
LN_EPS = 1e-5
NEG_INF = -1e30
LANES = 128
V7X_VMEM_BYTES = 64 * 1024 * 1024


def _vmem_limit(nbytes):
    return int(min(max(nbytes * 5 // 4 + (4 << 20), 16 << 20), V7X_VMEM_BYTES - (6 << 20)))


def _ln(x, g, b):
    mu = jnp.mean(x, axis=-1, keepdims=True)
    xc = x - mu
    var = jnp.mean(xc * xc, axis=-1, keepdims=True)
    return xc * lax.rsqrt(var + LN_EPS) * g + b


def _ln_cast_kernel(x_ref, g_ref, b_ref, o_ref):
    o_ref[...] = _ln(x_ref[...], g_ref[...], b_ref[...]).astype(o_ref.dtype)


def _ln_cast(x2d, g, b, tm):
    T, D = x2d.shape
    return pl.pallas_call(
        _ln_cast_kernel,
        grid=(T // tm,),
        in_specs=[pl.BlockSpec((tm, D), lambda i: (i, 0)),
                  pl.BlockSpec((1, D), lambda i: (0, 0)),
                  pl.BlockSpec((1, D), lambda i: (0, 0))],
        out_specs=pl.BlockSpec((tm, D), lambda i: (i, 0)),
        out_shape=jax.ShapeDtypeStruct((T, D), BF16),
        compiler_params=pltpu.CompilerParams(
            dimension_semantics=("parallel",),
            vmem_limit_bytes=_vmem_limit(2 * tm * D * 6)),
    )(x2d, g, b)


def _proj_nat_kernel(x_ref, w_ref, b_ref, o_ref):
    acc = jnp.dot(x_ref[...], w_ref[...], preferred_element_type=F32) + b_ref[...]
    o_ref[...] = acc.astype(o_ref.dtype)


def _proj_nat(xn, w, b, tm, tn):
    T, D = xn.shape
    N = w.shape[1]
    return pl.pallas_call(
        _proj_nat_kernel,
        grid=(T // tm, N // tn),
        in_specs=[pl.BlockSpec((tm, D), lambda m, n: (m, 0)),
                  pl.BlockSpec((D, tn), lambda m, n: (0, n)),
                  pl.BlockSpec((1, tn), lambda m, n: (0, n))],
        out_specs=pl.BlockSpec((tm, tn), lambda m, n: (m, n)),
        out_shape=jax.ShapeDtypeStruct((T, N), BF16),
        compiler_params=pltpu.CompilerParams(
            dimension_semantics=("parallel", "arbitrary"),
            vmem_limit_bytes=_vmem_limit(2 * (tm * D * 2 + D * tn * 2 + tm * tn * 2) + tm * tn * 4)),
    )(xn, w, b)


def _proj_heads_kernel(x_ref, w_ref, b_ref, o_ref, acc_ref, *, dil, q_blocks, q_scale):
    n = pl.program_id(2)
    nj, tm, _ = acc_ref.shape
    tn = nj * LANES
    scale = jnp.where(n < q_blocks, q_scale, 1.0).astype(F32)
    acc = (jnp.dot(x_ref[...], w_ref[...], preferred_element_type=F32) + b_ref[...]) * scale
    if dil == 1:
        for j in range(tn // LANES):
            o_ref[j, 0, 0] = acc[:, j * LANES:(j + 1) * LANES].astype(o_ref.dtype)
    else:
        for j in range(nj):
            acc_ref[j] = acc[:, j * LANES:(j + 1) * LANES]
        for j in range(nj):
            for r in range(dil):
                o_ref[j, 0, r] = acc_ref[j, pl.ds(r, tm // dil, stride=dil), :].astype(o_ref.dtype)


def _proj_heads(xn, w, b, B, S, dil, group_width, tm, tn):
    T, D = xn.shape
    N = w.shape[1]
    L = S // dil
    mt = S // tm
    kern = functools.partial(_proj_heads_kernel, dil=dil, q_blocks=group_width // tn,
                             q_scale=HEAD_DIM ** -0.5)
    return pl.pallas_call(
        kern,
        grid=(B, mt, N // tn),
        in_specs=[pl.BlockSpec((tm, D), lambda bb, m, n: (bb * mt + m, 0)),
                  pl.BlockSpec((D, tn), lambda bb, m, n: (0, n)),
                  pl.BlockSpec((1, tn), lambda bb, m, n: (0, n))],
        out_specs=pl.BlockSpec((tn // LANES, 1, dil, tm // dil, LANES),
                               lambda bb, m, n: (n, bb, 0, m, 0)),
        out_shape=jax.ShapeDtypeStruct((N // LANES, B, dil, L, LANES), BF16),
        scratch_shapes=[pltpu.VMEM((tn // LANES, tm, LANES), F32)],
        compiler_params=pltpu.CompilerParams(
            dimension_semantics=("parallel", "parallel", "arbitrary"),
            vmem_limit_bytes=_vmem_limit(2 * (tm * D * 2 + D * tn * 2 + tm * tn * 2) + 2 * tm * tn * 4)),
    )(xn, w, b)


def _conv_kernel(cur_ref, prev_ref, next_ref, w_ref, cb_ref, g_ref, b_ref, o_ref, ext_ref, y_ref, *, C):
    i = pl.program_id(1)
    last = pl.num_programs(1) - 1
    tp = cur_ref.shape[1]

    def glu(blk):
        a = blk[:, :C].astype(F32)
        gate = blk[:, C:].astype(F32)
        return a * (1.0 / (1.0 + jnp.exp(-gate)))

    ext_ref[0:CONV_HALO] = glu(prev_ref[0]) * jnp.where(i > 0, 1.0, 0.0)
    ext_ref[CONV_HALO:CONV_HALO + tp] = glu(cur_ref[0])
    ext_ref[CONV_HALO + tp:] = glu(next_ref[0]) * jnp.where(i < last, 1.0, 0.0)

    first = CONV_HALO - CONV_WIDTH // 2

    def chan_block(cb, carry):
        c0 = pl.multiple_of(cb * LANES, LANES)
        acc = jnp.zeros((tp, LANES), F32)
        for d in range(CONV_WIDTH):
            acc = acc + ext_ref[pl.ds(first + d, tp), pl.ds(c0, LANES)] * w_ref[pl.ds(d, 1), pl.ds(c0, LANES)]
        y_ref[:, pl.ds(c0, LANES)] = acc + cb_ref[:, pl.ds(c0, LANES)]
        return carry

    lax.fori_loop(0, C // LANES, chan_block, 0)
    y = _ln(y_ref[...], g_ref[...], b_ref[...])
    o_ref[0] = (y * (1.0 / (1.0 + jnp.exp(-y)))).astype(o_ref.dtype)


def _conv_module(ag, conv_w, conv_b, ln_g, ln_b, B, S, tp):
    C = ag.shape[1] // 2
    ag3 = ag.reshape(B, S, 2 * C)
    hb = tp // CONV_HALO
    nh = S // CONV_HALO
    out = pl.pallas_call(
        functools.partial(_conv_kernel, C=C),
        grid=(B, S // tp),
        in_specs=[pl.BlockSpec((1, tp, 2 * C), lambda b, i: (b, i, 0)),
                  pl.BlockSpec((1, CONV_HALO, 2 * C), lambda b, i: (b, jnp.maximum(i * hb - 1, 0), 0)),
                  pl.BlockSpec((1, CONV_HALO, 2 * C), lambda b, i: (b, jnp.minimum((i + 1) * hb, nh - 1), 0)),
                  pl.BlockSpec((CONV_WIDTH, C), lambda b, i: (0, 0)),
                  pl.BlockSpec((1, C), lambda b, i: (0, 0)),
                  pl.BlockSpec((1, C), lambda b, i: (0, 0)),
                  pl.BlockSpec((1, C), lambda b, i: (0, 0))],
        out_specs=pl.BlockSpec((1, tp, C), lambda b, i: (b, i, 0)),
        out_shape=jax.ShapeDtypeStruct((B, S, C), BF16),
        scratch_shapes=[pltpu.VMEM((tp + 2 * CONV_HALO, C), F32), pltpu.VMEM((tp, C), F32)],
        compiler_params=pltpu.CompilerParams(
            dimension_semantics=("parallel", "parallel"),
            vmem_limit_bytes=_vmem_limit(2 * tp * 2 * C * 2 + 2 * tp * C * 2 + 3 * tp * C * 4)),
    )(ag3, ag3, ag3, conv_w, conv_b, ln_g, ln_b)
    return out.reshape(B * S, C)


def _attn_kernel(slopes_ref, *refs, tp, dils, heads):
    ng = len(dils)
    ins = refs[:7 * ng]
    o_ref = refs[7 * ng]
    scr = refs[7 * ng + 1:]
    kwins, vwins = scr[:ng], scr[ng:2 * ng]
    o_nat, m_nat, l_nat = scr[2 * ng:2 * ng + 3]

    h = pl.program_id(1)
    i = pl.program_id(2)
    W = HALF_WINDOW

    for g, dil in enumerate(dils):
        q_ref, kc, kp, kn, vc, vp, vn = ins[7 * g:7 * g + 7]
        kwin, vwin = kwins[g], vwins[g]
        n = tp // dil
        L = n * pl.num_programs(2)
        sq = min(128, n)
        nsub = n // sq
        for win, cur, prev, nxt in ((kwin, kc, kp, kn), (vwin, vc, vp, vn)):
            win[:, 0:W] = prev[0, 0]
            win[:, W:W + n] = cur[0, 0]
            win[:, W + n:] = nxt[0, 0]

        slope = slopes_ref[g, h] * float(dil)
        row = lax.broadcasted_iota(jnp.int32, (sq, sq + 2 * W), 0)
        col = lax.broadcasted_iota(jnp.int32, (sq, sq + 2 * W), 1)
        dist = jnp.abs(col - W - row)
        bias = jnp.where(dist <= W, -slope * dist.astype(F32), NEG_INF)
        col1 = lax.broadcasted_iota(jnp.int32, (1, sq + 2 * W), 1)

        for r in range(dil):
            for sub in range(nsub):
                qs = q_ref[0, 0, r, sub * sq:(sub + 1) * sq, :]
                kw = kwin[r, sub * sq:sub * sq + sq + 2 * W, :]
                vw = vwin[r, sub * sq:sub * sq + sq + 2 * W, :]
                s = lax.dot_general(qs, kw, (((1,), (1,)), ((), ())), preferred_element_type=F32) + bias
                if sub == 0 or sub == nsub - 1:
                    kidx = i * n + (sub * sq - W) + col1
                    s = s + jnp.where((kidx >= 0) & (kidx < L), 0.0, NEG_INF)
                m = jnp.max(s, axis=-1, keepdims=True)
                p = jnp.exp(s - m)
                l = jnp.sum(p, axis=-1, keepdims=True)
                o = jnp.dot(p.astype(BF16), vw, preferred_element_type=F32)
                if dil == 1:
                    rows = pl.ds(sub * sq, sq)
                else:
                    rows = pl.ds(sub * sq * dil + r, sq, stride=dil)
                o_nat[g, rows, :] = o
                m_nat[g, rows, :] = jnp.broadcast_to(m, (sq, LANES))
                l_nat[g, rows, :] = jnp.broadcast_to(l, (sq, LANES))

    ch = min(128, tp)

    def merge(c, carry):
        r0 = pl.multiple_of(c * ch, ch)
        ms = [m_nat[g, pl.ds(r0, ch), :] for g in range(ng)]
        mx = functools.reduce(jnp.maximum, ms)
        num = jnp.zeros((ch, LANES), F32)
        den = jnp.zeros((ch, LANES), F32)
        for g in range(ng):
            wg = jnp.exp(ms[g] - mx)
            num = num + wg * o_nat[g, pl.ds(r0, ch), :]
            den = den + wg * l_nat[g, pl.ds(r0, ch), :]
        o_ref[0, pl.ds(r0, ch), :] = (num / den).astype(o_ref.dtype)
        return carry

    lax.fori_loop(0, tp // ch, merge, 0)


def _attention(qkvs, slopes, B, S, heads, tp):
    dils = tuple(d for _, d in ATT_GROUPS)
    W = HALF_WINDOW
    H = heads
    in_specs = [pl.BlockSpec(memory_space=pltpu.SMEM)]
    args = [slopes]
    scratch_k, scratch_v = [], []
    nbytes = 0
    for g, dil in enumerate(dils):
        n = tp // dil
        L = S // dil
        hb = n // W
        nh = L // W

        def cur_map(which):
            return lambda b, h, i: (which * H + h, b, 0, i, 0)

        def prev_map(which, hb=hb):
            return lambda b, h, i: (which * H + h, b, 0, jnp.maximum(i * hb - 1, 0), 0)

        def next_map(which, hb=hb, nh=nh):
            return lambda b, h, i: (which * H + h, b, 0, jnp.minimum((i + 1) * hb, nh - 1), 0)

        blk = (1, 1, dil, n, LANES)
        halo = (1, 1, dil, W, LANES)
        in_specs += [pl.BlockSpec(blk, cur_map(0)),
                     pl.BlockSpec(blk, cur_map(1)), pl.BlockSpec(halo, prev_map(1)), pl.BlockSpec(halo, next_map(1)),
                     pl.BlockSpec(blk, cur_map(2)), pl.BlockSpec(halo, prev_map(2)), pl.BlockSpec(halo, next_map(2))]
        args += [qkvs[g]] * 7
        scratch_k.append(pltpu.VMEM((dil, n + 2 * W, LANES), BF16))
        scratch_v.append(pltpu.VMEM((dil, n + 2 * W, LANES), BF16))
        nbytes += 2 * (3 * tp + 4 * dil * W) * LANES * 2 + 2 * (tp + 2 * W * dil) * LANES * 2
    nbytes += 9 * tp * LANES * 4 + 2 * tp * LANES * 2
    ng = len(dils)
    out = pl.pallas_call(
        functools.partial(_attn_kernel, tp=tp, dils=dils, heads=H),
        grid=(B, H, S // tp),
        in_specs=in_specs,
        out_specs=pl.BlockSpec((1, tp, LANES), lambda b, h, i: (b, i, h)),
        out_shape=jax.ShapeDtypeStruct((B, S, H * LANES), BF16),
        scratch_shapes=scratch_k + scratch_v + [pltpu.VMEM((ng, tp, LANES), F32)] * 3,
        compiler_params=pltpu.CompilerParams(
            dimension_semantics=("parallel", "parallel", "parallel"),
            vmem_limit_bytes=_vmem_limit(nbytes)),
    )(*args)
    return out.reshape(B * S, H * LANES)


def _outproj_kernel(*refs, alpha, tile_starts):
    nk = len(tile_starts) - 1
    trunk_refs = [refs[3 * k:3 * k + 3] for k in range(nk)]
    (lig_ref, lib_ref, wu_ref, wo_ref, bo_ref, g1_ref, b1_ref, wrh_ref, wrl_ref, br_ref,
     x1_ref, lg_ref, xn_ref, us_ref, os_ref) = refs[3 * nk:]
    m = pl.program_id(0)
    n = pl.program_id(1)
    tn = wu_ref.shape[1]

    for k, (x_ref, u_ref, o_ref) in enumerate(trunk_refs):
        @pl.when((n == 0) & (m >= tile_starts[k]) & (m < tile_starts[k + 1]))
        def _():
            xn_ref[...] = _ln(x_ref[...], lig_ref[...], lib_ref[...])
            us_ref[...] = u_ref[...]
            os_ref[...] = o_ref[...]

    c0 = pl.multiple_of(n * tn, LANES)
    mix = (jnp.dot(us_ref[...], wu_ref[...], preferred_element_type=F32)
           + jnp.dot(os_ref[...], wo_ref[...], preferred_element_type=F32) + bo_ref[...])
    x1_ref[:, pl.ds(c0, tn)] = alpha * xn_ref[:, pl.ds(c0, tn)] + mix

    @pl.when(n == pl.num_programs(1) - 1)
    def _():
        x1 = _ln(x1_ref[...], g1_ref[...], b1_ref[...])
        x1_ref[...] = x1
        xh = x1.astype(BF16)
        xl = (x1 - xh.astype(F32)).astype(BF16)
        wh = wrh_ref[...]
        lg_ref[...] = (jnp.dot(xh, wh, preferred_element_type=F32)
                       + jnp.dot(xl, wh, preferred_element_type=F32)
                       + jnp.dot(xh, wrl_ref[...], preferred_element_type=F32) + br_ref[...])


def _outproj(trunk_inputs, ln_in_g, ln_in_b, wu, wo, b_out, ln1_g, ln1_b, wrh, wrl, br, alpha, tm, tn):
    D = trunk_inputs[0][0].shape[1]
    C = wu.shape[0]
    GW = wo.shape[0]
    RL = wrh.shape[1]
    tile_starts = [0]
    for x, _, _ in trunk_inputs:
        tile_starts.append(tile_starts[-1] + x.shape[0] // tm)
    total_rows = tile_starts[-1] * tm
    const = lambda m, n: (0, 0)
    in_specs, args = [], []
    for k, (x, u, o) in enumerate(trunk_inputs):
        def rows(m, n, lo=tile_starts[k], cnt=tile_starts[k + 1] - tile_starts[k]):
            return (jnp.clip(m - lo, 0, cnt - 1), 0)
        in_specs += [pl.BlockSpec((tm, D), rows), pl.BlockSpec((tm, C), rows), pl.BlockSpec((tm, GW), rows)]
        args += [x, u, o]
    in_specs += [pl.BlockSpec((1, D), const), pl.BlockSpec((1, D), const),
                 pl.BlockSpec((C, tn), lambda m, n: (0, n)),
                 pl.BlockSpec((GW, tn), lambda m, n: (0, n)),
                 pl.BlockSpec((1, tn), lambda m, n: (0, n)),
                 pl.BlockSpec((1, D), const), pl.BlockSpec((1, D), const),
                 pl.BlockSpec((D, RL), const), pl.BlockSpec((D, RL), const), pl.BlockSpec((1, RL), const)]
    args += [ln_in_g, ln_in_b, wu, wo, b_out, ln1_g, ln1_b, wrh, wrl, br]
    nk = len(trunk_inputs)
    nbytes = (2 * (nk * (tm * D * 4 + tm * (C + GW) * 2) + (C + GW) * tn * 2 + tm * D * 4 + tm * RL * 4)
              + tm * D * 4 + tm * (C + GW) * 2 + 4 * D * RL * 2)
    return pl.pallas_call(
        functools.partial(_outproj_kernel, alpha=alpha, tile_starts=tuple(tile_starts)),
        grid=(tile_starts[-1], D // tn),
        in_specs=in_specs,
        out_specs=[pl.BlockSpec((tm, D), lambda m, n: (m, 0)),
                   pl.BlockSpec((tm, RL), lambda m, n: (m, 0))],
        out_shape=[jax.ShapeDtypeStruct((total_rows, D), F32),
                   jax.ShapeDtypeStruct((total_rows, RL), F32)],
        scratch_shapes=[pltpu.VMEM((tm, D), F32), pltpu.VMEM((tm, C), BF16), pltpu.VMEM((tm, GW), BF16)],
        compiler_params=pltpu.CompilerParams(
            dimension_semantics=("parallel", "arbitrary"),
            vmem_limit_bytes=_vmem_limit(nbytes)),
    )(*args)


def _start_row_gather(src_hbm, idx_ref, dst, sem, n_rows):
    def body(r, carry):
        pltpu.make_async_copy(src_hbm.at[pl.ds(idx_ref[0, 0, r], 1)], dst.at[pl.ds(r, 1)], sem).start()
        return carry
    lax.fori_loop(0, n_rows, body, 0, unroll=8)


def _wait_row_gather(src_hbm, dst, sem, n_rows):
    def body(r, carry):
        pltpu.make_async_copy(src_hbm.at[pl.ds(0, 1)], dst.at[pl.ds(r, 1)], sem).wait()
        return carry
    lax.fori_loop(0, n_rows, body, 0, unroll=8)


def _moe_up_kernel(te_ref, tv_ref, idx_ref, idxn_ref, x_hbm, wg_ref, wu_ref, h_ref, xbuf, sem):
    t = pl.program_id(0)
    nt = pl.num_programs(0)
    tm = xbuf.shape[1]
    slot = t % 2

    @pl.when((t == 0) & (tv_ref[0] > 0))
    def _():
        _start_row_gather(x_hbm, idx_ref, xbuf.at[0], sem.at[0], tm)

    nxt = jnp.minimum(t + 1, nt - 1)

    @pl.when((t + 1 < nt) & (tv_ref[nxt] > 0))
    def _():
        _start_row_gather(x_hbm, idxn_ref, xbuf.at[1 - slot], sem.at[1 - slot], tm)

    @pl.when(tv_ref[t] > 0)
    def _():
        _wait_row_gather(x_hbm, xbuf.at[slot], sem.at[slot], tm)
        xb = xbuf[slot].astype(BF16)
        a = jnp.dot(xb, wg_ref[0], preferred_element_type=F32)
        b = jnp.dot(xb, wu_ref[0], preferred_element_type=F32)
        h_ref[...] = (a * (1.0 / (1.0 + jnp.exp(-a))) * b).astype(h_ref.dtype)

    @pl.when(tv_ref[t] == 0)
    def _():
        h_ref[...] = jnp.zeros_like(h_ref)


def _moe_up(tile_e, tile_valid, buf_tok3, x1, wg, wu, tm):
    nt = tile_e.shape[0]
    E, D, F = wg.shape
    grid_spec = pltpu.PrefetchScalarGridSpec(
        num_scalar_prefetch=2,
        grid=(nt,),
        in_specs=[pl.BlockSpec((1, 1, tm), lambda t, te, tv: (t, 0, 0), memory_space=pltpu.SMEM),
                  pl.BlockSpec((1, 1, tm), lambda t, te, tv: (jnp.minimum(t + 1, nt - 1), 0, 0),
                               memory_space=pltpu.SMEM),
                  pl.BlockSpec(memory_space=pl.ANY),
                  pl.BlockSpec((1, D, F), lambda t, te, tv: (te[t], 0, 0)),
                  pl.BlockSpec((1, D, F), lambda t, te, tv: (te[t], 0, 0))],
        out_specs=pl.BlockSpec((tm, F), lambda t, te, tv: (t, 0)),
        scratch_shapes=[pltpu.VMEM((2, tm, D), F32), pltpu.SemaphoreType.DMA((2,))])
    return pl.pallas_call(
        _moe_up_kernel,
        grid_spec=grid_spec,
        out_shape=jax.ShapeDtypeStruct((nt * tm, F), BF16),
        compiler_params=pltpu.CompilerParams(
            dimension_semantics=("arbitrary",),
            vmem_limit_bytes=_vmem_limit(4 * D * F * 2 + 2 * tm * D * 4 + 2 * tm * F * 2 + tm * D * 2)),
    )(tile_e, tile_valid, buf_tok3, buf_tok3, x1, wg, wu)


def _moe_down_kernel(te_ref, tv_ref, h_ref, wd_ref, y_ref):
    t = pl.program_id(0)

    @pl.when(tv_ref[t] > 0)
    def _():
        y_ref[...] = jnp.dot(h_ref[...], wd_ref[0], preferred_element_type=F32)

    @pl.when(tv_ref[t] == 0)
    def _():
        y_ref[...] = jnp.zeros_like(y_ref)


def _moe_down(tile_e, tile_valid, hmid, wd, tm):
    nt = tile_e.shape[0]
    E, F, D = wd.shape
    grid_spec = pltpu.PrefetchScalarGridSpec(
        num_scalar_prefetch=2,
        grid=(nt,),
        in_specs=[pl.BlockSpec((tm, F), lambda t, te, tv: (t, 0)),
                  pl.BlockSpec((1, F, D), lambda t, te, tv: (te[t], 0, 0))],
        out_specs=pl.BlockSpec((tm, D), lambda t, te, tv: (t, 0)))
    return pl.pallas_call(
        _moe_down_kernel,
        grid_spec=grid_spec,
        out_shape=jax.ShapeDtypeStruct((nt * tm, D), F32),
        compiler_params=pltpu.CompilerParams(
            dimension_semantics=("arbitrary",),
            vmem_limit_bytes=_vmem_limit(2 * (F * D * 2 + tm * F * 2 + tm * D * 4))),
    )(tile_e, tile_valid, hmid, wd)


def _combine_kernel(idx_ref, idxn_ref, x1_ref, gate_ref, g_ref, b_ref, ys_hbm, o_ref, ybuf, sem, *, alpha):
    t = pl.program_id(0)
    nt = pl.num_programs(0)
    tm = x1_ref.shape[0]
    slot = t % 2

    @pl.when(t == 0)
    def _():
        _start_row_gather(ys_hbm, idx_ref, ybuf.at[0], sem.at[0], TOP_K * tm)

    @pl.when(t + 1 < nt)
    def _():
        _start_row_gather(ys_hbm, idxn_ref, ybuf.at[1 - slot], sem.at[1 - slot], TOP_K * tm)

    _wait_row_gather(ys_hbm, ybuf.at[slot], sem.at[slot], TOP_K * tm)
    gates = gate_ref[...]
    y0 = ybuf[slot, 0:tm, :]
    y1 = ybuf[slot, tm:2 * tm, :]
    ffn = gates[:, 0:1] * y0 + gates[:, 1:2] * y1
    o_ref[...] = _ln(alpha * x1_ref[...] + ffn, g_ref[...], b_ref[...])


def _combine(dest3, x1, row_offset, T, gates, ln_g, ln_b, ys, alpha, tm):
    D = x1.shape[1]
    nt = T // tm
    mo = row_offset // tm
    return pl.pallas_call(
        functools.partial(_combine_kernel, alpha=alpha),
        grid=(nt,),
        in_specs=[pl.BlockSpec((1, 1, TOP_K * tm), lambda t: (mo + t, 0, 0), memory_space=pltpu.SMEM),
                  pl.BlockSpec((1, 1, TOP_K * tm), lambda t: (mo + jnp.minimum(t + 1, nt - 1), 0, 0),
                               memory_space=pltpu.SMEM),
                  pl.BlockSpec((tm, D), lambda t: (mo + t, 0)),
                  pl.BlockSpec((tm, TOP_K), lambda t: (mo + t, 0)),
                  pl.BlockSpec((1, D), lambda t: (0, 0)),
                  pl.BlockSpec((1, D), lambda t: (0, 0)),
                  pl.BlockSpec(memory_space=pl.ANY)],
        out_specs=pl.BlockSpec((tm, D), lambda t: (t, 0)),
        out_shape=jax.ShapeDtypeStruct((T, D), F32),
        scratch_shapes=[pltpu.VMEM((2, TOP_K * tm, D), F32), pltpu.SemaphoreType.DMA((2,))],
        compiler_params=pltpu.CompilerParams(
            dimension_semantics=("arbitrary",),
            vmem_limit_bytes=_vmem_limit(4 * tm * D * 4 + 2 * TOP_K * tm * D * 4 + 2 * tm * D * 4)),
    )(dest3, dest3, x1, gates, ln_g, ln_b, ys)


def _route(logits, n_groups, n_experts, tile):
    T = logits.shape[0]
    per = n_experts // n_groups
    gl = logits[:, :n_groups]
    el = logits[:, n_groups:n_groups + n_experts].reshape(T, n_groups, per)
    g_star = jnp.argmax(gl, axis=-1)
    p_group = jnp.take_along_axis(jax.nn.softmax(gl, axis=-1), g_star[:, None], axis=1)
    in_group = jnp.take_along_axis(el, g_star[:, None, None], axis=1)[:, 0]
    top_v, top_i = lax.top_k(in_group, TOP_K)
    gates = jax.nn.softmax(top_v, axis=-1) * p_group
    flat_e = (g_star[:, None] * per + top_i).astype(jnp.int32).reshape(T * TOP_K)

    A = T * TOP_K
    onehot = (flat_e[:, None] == jnp.arange(n_experts, dtype=jnp.int32)[None, :]).astype(jnp.int32)
    csum = jnp.cumsum(onehot, axis=0)
    counts = csum[-1]
    rank = jnp.sum((csum - onehot) * onehot, axis=1)
    padded = (counts + tile - 1) // tile * tile
    pend = jnp.cumsum(padded)
    pstart = pend - padded
    dest = (pstart[flat_e] + rank).astype(jnp.int32)
    n_tiles = A // tile + n_experts
    flat_tok = jnp.arange(A, dtype=jnp.int32) // TOP_K
    buf_tok = jnp.zeros((n_tiles * tile,), jnp.int32).at[dest].set(flat_tok, unique_indices=True)
    tile_start = jnp.arange(n_tiles, dtype=jnp.int32) * tile
    tile_e = jnp.minimum(jnp.searchsorted(pend, tile_start, side='right'), n_experts - 1).astype(jnp.int32)
    tile_valid = (tile_start < pend[-1]).astype(jnp.int32)
    return gates, dest, buf_tok, tile_e, tile_valid


def _tiles(D, S, C, GW, F):
    max_dil = max(d for _, d in ATT_GROUPS)
    t = {}
    t['ln_rows'] = min(256, S)
    t['proj_rows'] = min(1024, S)
    t['proj_cols'] = min(512, GW)
    t['conv_rows'] = min(256, S)
    t['attn_rows'] = min(2048, S)
    t['out_rows'] = min(256, S)
    t['out_cols'] = min(1024, D)
    t['moe_rows'] = min(256, S)
    assert t['proj_rows'] % (16 * max_dil) == 0 and t['attn_rows'] % (HALF_WINDOW * max_dil) == 0
    assert S % t['proj_rows'] == 0 and S % t['attn_rows'] == 0 and S % t['conv_rows'] == 0
    return t


def kernel(x_prompt, x_sample, ln_in_g, ln_in_b, w_in, b_in, conv_w, conv_b, conv_ln_g, conv_ln_b, w_out, b_out,
           ln1_g, ln1_b, w_router_group, b_router_group, w_router_expert, b_router_expert, w_gate, w_up, w_down,
           ln2_g, ln2_b):
    depth = w_in.shape[0]
    assert depth == 1
    assert all(w // (2 * d) == HALF_WINDOW for w, d in ATT_GROUPS)
    alpha = (2.0 * depth) ** 0.25
    D = x_prompt.shape[-1]
    C = conv_w.shape[-1]
    n_att = len(ATT_GROUPS)
    GW = (w_in.shape[-1] - 2 * C) // (3 * n_att)
    H = GW // HEAD_DIM
    n_groups = w_router_group.shape[-1]
    n_experts = w_router_expert.shape[-1]
    F = w_gate.shape[-1]
    row = lambda v: v.reshape(1, -1).astype(F32)

    w_in_b = w_in[0].astype(BF16)
    b_in_r = row(b_in[0])
    w_sections = [(w_in_b[:, :2 * C], b_in_r[:, :2 * C])]
    off = 2 * C
    for _ in range(n_att):
        w_sections.append((w_in_b[:, off:off + 3 * GW], b_in_r[:, off:off + 3 * GW]))
        off += 3 * GW
    w_out_b = w_out[0].astype(BF16)
    wu_out, wo_out = w_out_b[:C], w_out_b[C:]
    RL = LANES
    w_r = jnp.concatenate([w_router_group[0], w_router_expert[0]], axis=1).astype(F32)
    w_r = jnp.pad(w_r, ((0, 0), (0, RL - w_r.shape[1])))
    wrh = w_r.astype(BF16)
    wrl = (w_r - wrh.astype(F32)).astype(BF16)
    b_r = jnp.pad(jnp.concatenate([b_router_group[0], b_router_expert[0]]).astype(F32),
                  (0, RL - n_groups - n_experts)).reshape(1, RL)
    wg_b, wu_b, wd_b = w_gate[0].astype(BF16), w_up[0].astype(BF16), w_down[0].astype(BF16)
    n_heads = n_att * H
    slopes = jnp.exp2(-8.0 * jnp.arange(1, n_heads + 1, dtype=F32) / n_heads).reshape(n_att, H)

    trunks = [x_prompt, x_sample]
    total = sum(x.shape[0] * x.shape[1] for x in trunks)
    staged = []
    offsets = []
    row_offset = 0
    for x in trunks:
        B, S, _ = x.shape
        T = B * S
        t = _tiles(D, S, C, GW, F)
        x2d = x.reshape(T, D)
        xn = _ln_cast(x2d, row(ln_in_g), row(ln_in_b), t['ln_rows'])
        ag = _proj_nat(xn, w_sections[0][0], w_sections[0][1], t['proj_rows'], min(t['proj_cols'], 2 * C))
        u = _conv_module(ag, conv_w[0].astype(F32), row(conv_b[0]), row(conv_ln_g[0]), row(conv_ln_b[0]),
                         B, S, t['conv_rows'])
        qkvs = [_proj_heads(xn, w_sections[1 + g][0], w_sections[1 + g][1], B, S, dil, GW,
                            t['proj_rows'], t['proj_cols'])
                for g, (_, dil) in enumerate(ATT_GROUPS)]
        o = _attention(qkvs, slopes, B, S, H, t['attn_rows'])
        staged.append((x2d, u, o))
        offsets.append((row_offset, T, t))
        row_offset += T
    t0 = offsets[0][2]
    x1, logits = _outproj(staged, row(ln_in_g), row(ln_in_b), wu_out, wo_out, row(b_out[0]),
                          row(ln1_g[0]), row(ln1_b[0]), wrh, wrl, b_r, alpha, t0['out_rows'], t0['out_cols'])

    tm = offsets[0][2]['moe_rows']
    gates, dest, buf_tok, tile_e, tile_valid = _route(logits, n_groups, n_experts, tm)
    nt = tile_e.shape[0]
    hmid = _moe_up(tile_e, tile_valid, buf_tok.reshape(nt, 1, tm), x1, wg_b, wu_b, tm)
    ys = _moe_down(tile_e, tile_valid, hmid, wd_b, tm)
    dest3 = dest.reshape(total // tm, tm, TOP_K).transpose(0, 2, 1).reshape(total // tm, 1, TOP_K * tm)
    outs = []
    for x, (ro, T, t) in zip(trunks, offsets):
        y = _combine(dest3, x1, ro, T, gates, row(ln2_g[0]), row(ln2_b[0]), ys, alpha, tm)
        outs.append(y.reshape(x.shape))
    return tuple(outs)
```

```python
import functools

import jax
import jax.numpy as jnp
from jax import lax
from jax.experimental import pallas as pl
from jax.experimental.pallas import tpu as pltpu

F32 = jnp.float32
BF16 = jnp.bfloat16

HEAD_DIM = 128
ATT_GROUPS = ((128, 1), (512, 4), (2048, 16))
HALF_WINDOW = 64
CONV_WIDTH = 31
CONV_HALO = 16
TOP_K = 2
LN_EPS = 1e-5
NEG_INF = -1e30
LANES = 128
V7X_VMEM_BYTES = 64 * 1024 * 1024


def _vmem_limit(nbytes):
    return int(min(max(nbytes * 5 // 4 + (4 << 20), 16 << 20), V7X_VMEM_BYTES - (6 << 20)))


def _ln(x, g, b):
    mu = jnp.mean(x, axis=-1, keepdims=True)
    xc = x - mu
    var = jnp.mean(xc * xc, axis=-1, keepdims=True)
    return xc * lax.rsqrt(var + LN_EPS) * g + b


def _ln_cast_kernel(x_ref, g_ref, b_ref, o_ref):
    o_ref[...] = _ln(x_ref[...], g_ref[...], b_ref[...]).astype(o_ref.dtype)


def _ln_cast(x2d, g, b, tm):
    T, D = x2d.shape
    return pl.pallas_call(
        _ln_cast_kernel, name="ln_cast",
        grid=(T // tm,),
        in_specs=[pl.BlockSpec((tm, D), lambda i: (i, 0)),
                  pl.BlockSpec((1, D), lambda i: (0, 0)),
                  pl.BlockSpec((1, D), lambda i: (0, 0))],
        out_specs=pl.BlockSpec((tm, D), lambda i: (i, 0)),
        out_shape=jax.ShapeDtypeStruct((T, D), BF16),
        compiler_params=pltpu.CompilerParams(
            dimension_semantics=("parallel",),
            vmem_limit_bytes=_vmem_limit(2 * tm * D * 6)),
    )(x2d, g, b)


def _proj_nat_kernel(x_ref, w_ref, b_ref, o_ref):
    acc = jnp.dot(x_ref[...], w_ref[...], preferred_element_type=F32) + b_ref[...]
    o_ref[...] = acc.astype(o_ref.dtype)


def _proj_nat(xn, w, b, tm, tn):
    T, D = xn.shape
    N = w.shape[1]
    return pl.pallas_call(
        _proj_nat_kernel, name="proj_conv",
        grid=(T // tm, N // tn),
        in_specs=[pl.BlockSpec((tm, D), lambda m, n: (m, 0)),
                  pl.BlockSpec((D, tn), lambda m, n: (0, n)),
                  pl.BlockSpec((1, tn), lambda m, n: (0, n))],
        out_specs=pl.BlockSpec((tm, tn), lambda m, n: (m, n)),
        out_shape=jax.ShapeDtypeStruct((T, N), BF16),
        compiler_params=pltpu.CompilerParams(
            dimension_semantics=("parallel", "arbitrary"),
            vmem_limit_bytes=_vmem_limit(2 * (tm * D * 2 + D * tn * 2 + tm * tn * 2) + tm * tn * 4)),
    )(xn, w, b)


def _proj_heads_kernel(x_ref, w_ref, b_ref, o_ref, acc_ref, *, dil, q_blocks, q_scale):
    n = pl.program_id(2)
    nj, tm, _ = acc_ref.shape
    tn = nj * LANES
    scale = jnp.where(n < q_blocks, q_scale, 1.0).astype(F32)
    acc = (jnp.dot(x_ref[...], w_ref[...], preferred_element_type=F32) + b_ref[...]) * scale
    if dil == 1:
        for j in range(tn // LANES):
            o_ref[j, 0, 0] = acc[:, j * LANES:(j + 1) * LANES].astype(o_ref.dtype)
    else:
        for j in range(nj):
            acc_ref[j] = acc[:, j * LANES:(j + 1) * LANES]
        for j in range(nj):
            for r in range(dil):
                o_ref[j, 0, r] = acc_ref[j, pl.ds(r, tm // dil, stride=dil), :].astype(o_ref.dtype)


def _proj_heads(xn, w, b, B, S, dil, group_width, tm, tn):
    T, D = xn.shape
    N = w.shape[1]
    L = S // dil
    mt = S // tm
    kern = functools.partial(_proj_heads_kernel, dil=dil, q_blocks=group_width // tn,
                             q_scale=HEAD_DIM ** -0.5)
    return pl.pallas_call(
        kern, name=f"proj_attn_dil{dil}",
        grid=(B, mt, N // tn),
        in_specs=[pl.BlockSpec((tm, D), lambda bb, m, n: (bb * mt + m, 0)),
                  pl.BlockSpec((D, tn), lambda bb, m, n: (0, n)),
                  pl.BlockSpec((1, tn), lambda bb, m, n: (0, n))],
        out_specs=pl.BlockSpec((tn // LANES, 1, dil, tm // dil, LANES),
                               lambda bb, m, n: (n, bb, 0, m, 0)),
        out_shape=jax.ShapeDtypeStruct((N // LANES, B, dil, L, LANES), BF16),
        scratch_shapes=[pltpu.VMEM((tn // LANES, tm, LANES), F32)],
        compiler_params=pltpu.CompilerParams(
            dimension_semantics=("parallel", "parallel", "arbitrary"),
            vmem_limit_bytes=_vmem_limit(2 * (tm * D * 2 + D * tn * 2 + tm * tn * 2) + 2 * tm * tn * 4)),
    )(xn, w, b)


def _conv_kernel(cur_ref, prev_ref, next_ref, w_ref, cb_ref, g_ref, b_ref, o_ref, ext_ref, y_ref, *, C):
    i = pl.program_id(1)
    last = pl.num_programs(1) - 1
    tp = cur_ref.shape[1]

    def glu(blk):
        a = blk[:, :C].astype(F32)
        gate = blk[:, C:].astype(F32)
        return a * (1.0 / (1.0 + jnp.exp(-gate)))

    ext_ref[0:CONV_HALO] = glu(prev_ref[0]) * jnp.where(i > 0, 1.0, 0.0)
    ext_ref[CONV_HALO:CONV_HALO + tp] = glu(cur_ref[0])
    ext_ref[CONV_HALO + tp:] = glu(next_ref[0]) * jnp.where(i < last, 1.0, 0.0)

    first = CONV_HALO - CONV_WIDTH // 2

    def chan_block(cb, carry):
        c0 = pl.multiple_of(cb * LANES, LANES)
        acc = jnp.zeros((tp, LANES), F32)
        for d in range(CONV_WIDTH):
            acc = acc + ext_ref[pl.ds(first + d, tp), pl.ds(c0, LANES)] * w_ref[pl.ds(d, 1), pl.ds(c0, LANES)]
        y_ref[:, pl.ds(c0, LANES)] = acc + cb_ref[:, pl.ds(c0, LANES)]
        return carry

    lax.fori_loop(0, C // LANES, chan_block, 0)
    y = _ln(y_ref[...], g_ref[...], b_ref[...])
    o_ref[0] = (y * (1.0 / (1.0 + jnp.exp(-y)))).astype(o_ref.dtype)


def _conv_module(ag, conv_w, conv_b, ln_g, ln_b, B, S, tp):
    C = ag.shape[1] // 2
    ag3 = ag.reshape(B, S, 2 * C)
    hb = tp // CONV_HALO
    nh = S // CONV_HALO
    out = pl.pallas_call(
        functools.partial(_conv_kernel, C=C), name="conv_module",
        grid=(B, S // tp),
        in_specs=[pl.BlockSpec((1, tp, 2 * C), lambda b, i: (b, i, 0)),
                  pl.BlockSpec((1, CONV_HALO, 2 * C), lambda b, i: (b, jnp.maximum(i * hb - 1, 0), 0)),
                  pl.BlockSpec((1, CONV_HALO, 2 * C), lambda b, i: (b, jnp.minimum((i + 1) * hb, nh - 1), 0)),
                  pl.BlockSpec((CONV_WIDTH, C), lambda b, i: (0, 0)),
                  pl.BlockSpec((1, C), lambda b, i: (0, 0)),
                  pl.BlockSpec((1, C), lambda b, i: (0, 0)),
                  pl.BlockSpec((1, C), lambda b, i: (0, 0))],
        out_specs=pl.BlockSpec((1, tp, C), lambda b, i: (b, i, 0)),
        out_shape=jax.ShapeDtypeStruct((B, S, C), BF16),
        scratch_shapes=[pltpu.VMEM((tp + 2 * CONV_HALO, C), F32), pltpu.VMEM((tp, C), F32)],
        compiler_params=pltpu.CompilerParams(
            dimension_semantics=("parallel", "parallel"),
            vmem_limit_bytes=_vmem_limit(2 * tp * 2 * C * 2 + 2 * tp * C * 2 + 3 * tp * C * 4)),
    )(ag3, ag3, ag3, conv_w, conv_b, ln_g, ln_b)
    return out.reshape(B * S, C)


def _attn_kernel(slopes_ref, *refs, tp, dils, heads):
    ng = len(dils)
    ins = refs[:7 * ng]
    o_ref = refs[7 * ng]
    scr = refs[7 * ng + 1:]
    kwins, vwins = scr[:ng], scr[ng:2 * ng]
    o_nat, m_nat, l_nat = scr[2 * ng:2 * ng + 3]

    h = pl.program_id(1)
    i = pl.program_id(2)
    W = HALF_WINDOW

    for g, dil in enumerate(dils):
        q_ref, kc, kp, kn, vc, vp, vn = ins[7 * g:7 * g + 7]
        kwin, vwin = kwins[g], vwins[g]
        n = tp // dil
        L = n * pl.num_programs(2)
        sq = min(128, n)
        nsub = n // sq
        for win, cur, prev, nxt in ((kwin, kc, kp, kn), (vwin, vc, vp, vn)):
            win[:, 0:W] = prev[0, 0]
            win[:, W:W + n] = cur[0, 0]
            win[:, W + n:] = nxt[0, 0]

        slope = slopes_ref[g, h] * float(dil)
        row = lax.broadcasted_iota(jnp.int32, (sq, sq + 2 * W), 0)
        col = lax.broadcasted_iota(jnp.int32, (sq, sq + 2 * W), 1)
        dist = jnp.abs(col - W - row)
        bias = jnp.where(dist <= W, -slope * dist.astype(F32), NEG_INF)
        col1 = lax.broadcasted_iota(jnp.int32, (1, sq + 2 * W), 1)

        for r in range(dil):
            for sub in range(nsub):
                qs = q_ref[0, 0, r, sub * sq:(sub + 1) * sq, :]
                kw = kwin[r, sub * sq:sub * sq + sq + 2 * W, :]
                vw = vwin[r, sub * sq:sub * sq + sq + 2 * W, :]
                s = lax.dot_general(qs, kw, (((1,), (1,)), ((), ())), preferred_element_type=F32) + bias
                if sub == 0 or sub == nsub - 1:
                    kidx = i * n + (sub * sq - W) + col1
                    s = s + jnp.where((kidx >= 0) & (kidx < L), 0.0, NEG_INF)
                m = jnp.max(s, axis=-1, keepdims=True)
                p = jnp.exp(s - m)
                l = jnp.sum(p, axis=-1, keepdims=True)
                o = jnp.dot(p.astype(BF16), vw, preferred_element_type=F32)
                if dil == 1:
                    rows = pl.ds(sub * sq, sq)
                else:
                    rows = pl.ds(sub * sq * dil + r, sq, stride=dil)
                o_nat[g, rows, :] = o
                m_nat[g, rows, :] = jnp.broadcast_to(m, (sq, LANES))
                l_nat[g, rows, :] = jnp.broadcast_to(l, (sq, LANES))

    ch = min(128, tp)

    def merge(c, carry):
        r0 = pl.multiple_of(c * ch, ch)
        ms = [m_nat[g, pl.ds(r0, ch), :] for g in range(ng)]
        mx = functools.reduce(jnp.maximum, ms)
        num = jnp.zeros((ch, LANES), F32)
        den = jnp.zeros((ch, LANES), F32)
        for g in range(ng):
            wg = jnp.exp(ms[g] - mx)
            num = num + wg * o_nat[g, pl.ds(r0, ch), :]
            den = den + wg * l_nat[g, pl.ds(r0, ch), :]
        o_ref[0, pl.ds(r0, ch), :] = (num / den).astype(o_ref.dtype)
        return carry

    lax.fori_loop(0, tp // ch, merge, 0)


def _attention(qkvs, slopes, B, S, heads, tp):
    dils = tuple(d for _, d in ATT_GROUPS)
    W = HALF_WINDOW
    H = heads
    in_specs = [pl.BlockSpec(memory_space=pltpu.SMEM)]
    args = [slopes]
    scratch_k, scratch_v = [], []
    nbytes = 0
    for g, dil in enumerate(dils):
        n = tp // dil
        L = S // dil
        hb = n // W
        nh = L // W

        def cur_map(which):
            return lambda b, h, i: (which * H + h, b, 0, i, 0)

        def prev_map(which, hb=hb):
            return lambda b, h, i: (which * H + h, b, 0, jnp.maximum(i * hb - 1, 0), 0)

        def next_map(which, hb=hb, nh=nh):
            return lambda b, h, i: (which * H + h, b, 0, jnp.minimum((i + 1) * hb, nh - 1), 0)

        blk = (1, 1, dil, n, LANES)
        halo = (1, 1, dil, W, LANES)
        in_specs += [pl.BlockSpec(blk, cur_map(0)),
                     pl.BlockSpec(blk, cur_map(1)), pl.BlockSpec(halo, prev_map(1)), pl.BlockSpec(halo, next_map(1)),
                     pl.BlockSpec(blk, cur_map(2)), pl.BlockSpec(halo, prev_map(2)), pl.BlockSpec(halo, next_map(2))]
        args += [qkvs[g]] * 7
        scratch_k.append(pltpu.VMEM((dil, n + 2 * W, LANES), BF16))
        scratch_v.append(pltpu.VMEM((dil, n + 2 * W, LANES), BF16))
        nbytes += 2 * (3 * tp + 4 * dil * W) * LANES * 2 + 2 * (tp + 2 * W * dil) * LANES * 2
    nbytes += 9 * tp * LANES * 4 + 2 * tp * LANES * 2
    ng = len(dils)
    out = pl.pallas_call(
        functools.partial(_attn_kernel, tp=tp, dils=dils, heads=H), name="attention",
        grid=(B, H, S // tp),
        in_specs=in_specs,
        out_specs=pl.BlockSpec((1, tp, LANES), lambda b, h, i: (b, i, h)),
        out_shape=jax.ShapeDtypeStruct((B, S, H * LANES), BF16),
        scratch_shapes=scratch_k + scratch_v + [pltpu.VMEM((ng, tp, LANES), F32)] * 3,
        compiler_params=pltpu.CompilerParams(
            dimension_semantics=("parallel", "parallel", "parallel"),
            vmem_limit_bytes=_vmem_limit(nbytes)),
    )(*args)
    return out.reshape(B * S, H * LANES)


def _outproj_kernel(*refs, alpha, tile_starts):
    nk = len(tile_starts) - 1
    trunk_refs = [refs[3 * k:3 * k + 3] for k in range(nk)]
    (lig_ref, lib_ref, wu_ref, wo_ref, bo_ref, g1_ref, b1_ref, wrh_ref, wrl_ref, br_ref,
     x1_ref, lg_ref, xn_ref, us_ref, os_ref) = refs[3 * nk:]
    m = pl.program_id(0)
    n = pl.program_id(1)
    tn = wu_ref.shape[1]

    for k, (x_ref, u_ref, o_ref) in enumerate(trunk_refs):
        @pl.when((n == 0) & (m >= tile_starts[k]) & (m < tile_starts[k + 1]))
        def _():
            xn_ref[...] = _ln(x_ref[...], lig_ref[...], lib_ref[...])
            us_ref[...] = u_ref[...]
            os_ref[...] = o_ref[...]

    c0 = pl.multiple_of(n * tn, LANES)
    mix = (jnp.dot(us_ref[...], wu_ref[...], preferred_element_type=F32)
           + jnp.dot(os_ref[...], wo_ref[...], preferred_element_type=F32) + bo_ref[...])
    x1_ref[:, pl.ds(c0, tn)] = alpha * xn_ref[:, pl.ds(c0, tn)] + mix

    @pl.when(n == pl.num_programs(1) - 1)
    def _():
        x1 = _ln(x1_ref[...], g1_ref[...], b1_ref[...])
        x1_ref[...] = x1
        xh = x1.astype(BF16)
        xl = (x1 - xh.astype(F32)).astype(BF16)
        wh = wrh_ref[...]
        lg_ref[...] = (jnp.dot(xh, wh, preferred_element_type=F32)
                       + jnp.dot(xl, wh, preferred_element_type=F32)
                       + jnp.dot(xh, wrl_ref[...], preferred_element_type=F32) + br_ref[...])


def _outproj(trunk_inputs, ln_in_g, ln_in_b, wu, wo, b_out, ln1_g, ln1_b, wrh, wrl, br, alpha, tm, tn):
    D = trunk_inputs[0][0].shape[1]
    C = wu.shape[0]
    GW = wo.shape[0]
    RL = wrh.shape[1]
    tile_starts = [0]
    for x, _, _ in trunk_inputs:
        tile_starts.append(tile_starts[-1] + x.shape[0] // tm)
    total_rows = tile_starts[-1] * tm
    const = lambda m, n: (0, 0)
    in_specs, args = [], []
    for k, (x, u, o) in enumerate(trunk_inputs):
        def rows(m, n, lo=tile_starts[k], cnt=tile_starts[k + 1] - tile_starts[k]):
            return (jnp.clip(m - lo, 0, cnt - 1), 0)
        in_specs += [pl.BlockSpec((tm, D), rows), pl.BlockSpec((tm, C), rows), pl.BlockSpec((tm, GW), rows)]
        args += [x, u, o]
    in_specs += [pl.BlockSpec((1, D), const), pl.BlockSpec((1, D), const),
                 pl.BlockSpec((C, tn), lambda m, n: (0, n)),
                 pl.BlockSpec((GW, tn), lambda m, n: (0, n)),
                 pl.BlockSpec((1, tn), lambda m, n: (0, n)),
                 pl.BlockSpec((1, D), const), pl.BlockSpec((1, D), const),
                 pl.BlockSpec((D, RL), const), pl.BlockSpec((D, RL), const), pl.BlockSpec((1, RL), const)]
    args += [ln_in_g, ln_in_b, wu, wo, b_out, ln1_g, ln1_b, wrh, wrl, br]
    nk = len(trunk_inputs)
    nbytes = (2 * (nk * (tm * D * 4 + tm * (C + GW) * 2) + (C + GW) * tn * 2 + tm * D * 4 + tm * RL * 4)
              + tm * D * 4 + tm * (C + GW) * 2 + 4 * D * RL * 2)
    return pl.pallas_call(
        functools.partial(_outproj_kernel, alpha=alpha, tile_starts=tuple(tile_starts)), name="outproj",
        grid=(tile_starts[-1], D // tn),
        in_specs=in_specs,
        out_specs=[pl.BlockSpec((tm, D), lambda m, n: (m, 0)),
                   pl.BlockSpec((tm, RL), lambda m, n: (m, 0))],
        out_shape=[jax.ShapeDtypeStruct((total_rows, D), F32),
                   jax.ShapeDtypeStruct((total_rows, RL), F32)],
        scratch_shapes=[pltpu.VMEM((tm, D), F32), pltpu.VMEM((tm, C), BF16), pltpu.VMEM((tm, GW), BF16)],
        compiler_params=pltpu.CompilerParams(
            dimension_semantics=("parallel", "arbitrary"),
            vmem_limit_bytes=_vmem_limit(nbytes)),
    )(*args)


def _start_row_gather(src_hbm, idx_ref, dst, sem, n_rows):
    for r in range(n_rows):
        pltpu.make_async_copy(src_hbm.at[pl.ds(idx_ref[0, 0, r], 1)], dst.at[pl.ds(r, 1)], sem).start()


def _wait_row_gather(src_hbm, dst, sem, n_rows):
    for r in range(n_rows):
        pltpu.make_async_copy(src_hbm.at[pl.ds(0, 1)], dst.at[pl.ds(r, 1)], sem).wait()


def _dispatch_kernel(idx_ref, x_hbm, xs_hbm, sem):
    t = pl.program_id(0)
    nt = pl.num_programs(0)
    tm = idx_ref.shape[2]
    slot = t % 2
    _start_row_gather(x_hbm, idx_ref, xs_hbm.at[pl.ds(t * tm, tm)], sem.at[slot], tm)

    @pl.when(t > 0)
    def _():
        _wait_row_gather(x_hbm, xs_hbm.at[pl.ds(0, tm)], sem.at[1 - slot], tm)

    @pl.when(t == nt - 1)
    def _():
        _wait_row_gather(x_hbm, xs_hbm.at[pl.ds(0, tm)], sem.at[slot], tm)


def _dispatch(buf_tok3, x1):
    nt, _, tm = buf_tok3.shape
    D = x1.shape[1]
    return pl.pallas_call(
        _dispatch_kernel, name="moe_dispatch",
        grid=(nt,),
        in_specs=[pl.BlockSpec((1, 1, tm), lambda t: (t, 0, 0), memory_space=pltpu.SMEM),
                  pl.BlockSpec(memory_space=pl.ANY)],
        out_specs=pl.BlockSpec(memory_space=pl.ANY),
        out_shape=jax.ShapeDtypeStruct((nt * tm, D), x1.dtype),
        scratch_shapes=[pltpu.SemaphoreType.DMA((2,))],
        compiler_params=pltpu.CompilerParams(dimension_semantics=("arbitrary",)),
    )(buf_tok3, x1)


def _expert_changed(te_ref, t):
    return (t == 0) | (te_ref[t] != te_ref[jnp.maximum(t - 1, 0)])


def _moe_up_kernel(te_ref, tv_ref, x_ref, wg_ref, wu_ref, h_ref, wgb_ref, wub_ref):
    t = pl.program_id(1)
    valid = tv_ref[t] > 0

    @pl.when(valid & _expert_changed(te_ref, t))
    def _():
        wgb_ref[...] = wg_ref[0].astype(BF16)
        wub_ref[...] = wu_ref[0].astype(BF16)

    @pl.when(valid)
    def _():
        xb = x_ref[...].astype(BF16)
        a = jnp.dot(xb, wgb_ref[...], preferred_element_type=F32)
        b = jnp.dot(xb, wub_ref[...], preferred_element_type=F32)
        h_ref[...] = (a * (1.0 / (1.0 + jnp.exp(-a))) * b).astype(h_ref.dtype)

    @pl.when(jnp.logical_not(valid))
    def _():
        h_ref[...] = jnp.zeros_like(h_ref)


def _moe_up(tile_e, tile_valid, xs, wg, wu, tm, fc):
    nt = tile_e.shape[0]
    E, D, F = wg.shape
    grid_spec = pltpu.PrefetchScalarGridSpec(
        num_scalar_prefetch=2,
        grid=(F // fc, nt),
        in_specs=[pl.BlockSpec((tm, D), lambda c, t, te, tv: (t, 0)),
                  pl.BlockSpec((1, D, fc), lambda c, t, te, tv: (te[t], 0, c)),
                  pl.BlockSpec((1, D, fc), lambda c, t, te, tv: (te[t], 0, c))],
        out_specs=pl.BlockSpec((tm, fc), lambda c, t, te, tv: (t, c)),
        scratch_shapes=[pltpu.VMEM((D, fc), BF16), pltpu.VMEM((D, fc), BF16)])
    return pl.pallas_call(
        _moe_up_kernel, name="moe_up",
        grid_spec=grid_spec,
        out_shape=jax.ShapeDtypeStruct((nt * tm, F), BF16),
        compiler_params=pltpu.CompilerParams(
            dimension_semantics=("arbitrary", "arbitrary"),
            vmem_limit_bytes=_vmem_limit(2 * (tm * D * 4 + 2 * D * fc * 4 + tm * fc * 2) + 2 * D * fc * 2
                                         + tm * D * 2)),
    )(tile_e, tile_valid, xs, wg, wu)


def _moe_down_kernel(te_ref, tv_ref, h_ref, wd_ref, y_ref, wdb_ref):
    t = pl.program_id(0)
    valid = tv_ref[t] > 0

    @pl.when(valid & _expert_changed(te_ref, t))
    def _():
        wdb_ref[...] = wd_ref[0].astype(BF16)

    @pl.when(valid)
    def _():
        y_ref[...] = jnp.dot(h_ref[...], wdb_ref[...], preferred_element_type=F32)

    @pl.when(jnp.logical_not(valid))
    def _():
        y_ref[...] = jnp.zeros_like(y_ref)


def _moe_down(tile_e, tile_valid, hmid, wd, tm):
    nt = tile_e.shape[0]
    E, F, D = wd.shape
    grid_spec = pltpu.PrefetchScalarGridSpec(
        num_scalar_prefetch=2,
        grid=(nt,),
        in_specs=[pl.BlockSpec((tm, F), lambda t, te, tv: (t, 0)),
                  pl.BlockSpec((1, F, D), lambda t, te, tv: (te[t], 0, 0))],
        out_specs=pl.BlockSpec((tm, D), lambda t, te, tv: (t, 0)),
        scratch_shapes=[pltpu.VMEM((F, D), BF16)])
    return pl.pallas_call(
        _moe_down_kernel, name="moe_down",
        grid_spec=grid_spec,
        out_shape=jax.ShapeDtypeStruct((nt * tm, D), F32),
        compiler_params=pltpu.CompilerParams(
            dimension_semantics=("arbitrary",),
            vmem_limit_bytes=_vmem_limit(2 * (F * D * 4 + tm * F * 2 + tm * D * 4) + F * D * 2)),
    )(tile_e, tile_valid, hmid, wd)


def _combine_kernel(idx_ref, idxn_ref, x1_ref, gate_ref, g_ref, b_ref, ys_hbm, o_ref, ybuf, sem, *, alpha):
    t = pl.program_id(0)
    nt = pl.num_programs(0)
    tm = x1_ref.shape[0]
    slot = t % 2

    @pl.when(t == 0)
    def _():
        _start_row_gather(ys_hbm, idx_ref, ybuf.at[0], sem.at[0], TOP_K * tm)

    @pl.when(t + 1 < nt)
    def _():
        _start_row_gather(ys_hbm, idxn_ref, ybuf.at[1 - slot], sem.at[1 - slot], TOP_K * tm)

    _wait_row_gather(ys_hbm, ybuf.at[slot], sem.at[slot], TOP_K * tm)
    gates = gate_ref[...]
    y0 = ybuf[slot, 0:tm, :]
    y1 = ybuf[slot, tm:2 * tm, :]
    ffn = gates[:, 0:1] * y0 + gates[:, 1:2] * y1
    o_ref[...] = _ln(alpha * x1_ref[...] + ffn, g_ref[...], b_ref[...])


def _combine(dest3, x1, row_offset, T, gates, ln_g, ln_b, ys, alpha, tm):
    D = x1.shape[1]
    nt = T // tm
    mo = row_offset // tm
    return pl.pallas_call(
        functools.partial(_combine_kernel, alpha=alpha), name="moe_combine",
        grid=(nt,),
        in_specs=[pl.BlockSpec((1, 1, TOP_K * tm), lambda t: (mo + t, 0, 0), memory_space=pltpu.SMEM),
                  pl.BlockSpec((1, 1, TOP_K * tm), lambda t: (mo + jnp.minimum(t + 1, nt - 1), 0, 0),
                               memory_space=pltpu.SMEM),
                  pl.BlockSpec((tm, D), lambda t: (mo + t, 0)),
                  pl.BlockSpec((tm, TOP_K), lambda t: (mo + t, 0)),
                  pl.BlockSpec((1, D), lambda t: (0, 0)),
                  pl.BlockSpec((1, D), lambda t: (0, 0)),
                  pl.BlockSpec(memory_space=pl.ANY)],
        out_specs=pl.BlockSpec((tm, D), lambda t: (t, 0)),
        out_shape=jax.ShapeDtypeStruct((T, D), F32),
        scratch_shapes=[pltpu.VMEM((2, TOP_K * tm, D), F32), pltpu.SemaphoreType.DMA((2,))],
        compiler_params=pltpu.CompilerParams(
            dimension_semantics=("arbitrary",),
            vmem_limit_bytes=_vmem_limit(4 * tm * D * 4 + 2 * TOP_K * tm * D * 4 + 2 * tm * D * 4)),
    )(dest3, dest3, x1, gates, ln_g, ln_b, ys)


def _route(logits, n_groups, n_experts, tile):
    T = logits.shape[0]
    per = n_experts // n_groups
    gl = logits[:, :n_groups]
    el = logits[:, n_groups:n_groups + n_experts].reshape(T, n_groups, per)
    g_star = jnp.argmax(gl, axis=-1)
    p_group = jnp.take_along_axis(jax.nn.softmax(gl, axis=-1), g_star[:, None], axis=1)
    in_group = jnp.take_along_axis(el, g_star[:, None, None], axis=1)[:, 0]
    top_v, top_i = lax.top_k(in_group, TOP_K)
    gates = jax.nn.softmax(top_v, axis=-1) * p_group
    flat_e = (g_star[:, None] * per + top_i).astype(jnp.int32).reshape(T * TOP_K)

    A = T * TOP_K
    onehot = (flat_e[:, None] == jnp.arange(n_experts, dtype=jnp.int32)[None, :]).astype(jnp.int32)
    csum = jnp.cumsum(onehot, axis=0)
    counts = csum[-1]
    rank = jnp.sum((csum - onehot) * onehot, axis=1)
    padded = (counts + tile - 1) // tile * tile
    pend = jnp.cumsum(padded)
    pstart = pend - padded
    dest = (pstart[flat_e] + rank).astype(jnp.int32)
    n_tiles = A // tile + n_experts
    flat_tok = jnp.arange(A, dtype=jnp.int32) // TOP_K
    buf_tok = jnp.zeros((n_tiles * tile,), jnp.int32).at[dest].set(flat_tok, unique_indices=True)
    tile_start = jnp.arange(n_tiles, dtype=jnp.int32) * tile
    tile_e = jnp.minimum(jnp.searchsorted(pend, tile_start, side='right'), n_experts - 1).astype(jnp.int32)
    tile_valid = (tile_start < pend[-1]).astype(jnp.int32)
    return gates, dest, buf_tok, tile_e, tile_valid


def _tiles(D, S, C, GW, F):
    max_dil = max(d for _, d in ATT_GROUPS)
    t = {}
    t['ln_rows'] = min(256, S)
    t['proj_rows'] = min(1024, S)
    t['proj_cols'] = min(512, GW)
    t['conv_rows'] = min(256, S)
    t['attn_rows'] = min(2048, S)
    t['out_rows'] = min(256, S)
    t['out_cols'] = min(1024, D)
    t['moe_rows'] = min(256, S)
    t['moe_cols'] = 512
    assert t['proj_rows'] % (16 * max_dil) == 0 and t['attn_rows'] % (HALF_WINDOW * max_dil) == 0
    assert S % t['proj_rows'] == 0 and S % t['attn_rows'] == 0 and S % t['conv_rows'] == 0
    return t


def kernel(x_prompt, x_sample, ln_in_g, ln_in_b, w_in, b_in, conv_w, conv_b, conv_ln_g, conv_ln_b, w_out, b_out,
           ln1_g, ln1_b, w_router_group, b_router_group, w_router_expert, b_router_expert, w_gate, w_up, w_down,
           ln2_g, ln2_b):
    depth = w_in.shape[0]
    assert depth == 1
    assert all(w // (2 * d) == HALF_WINDOW for w, d in ATT_GROUPS)
    alpha = (2.0 * depth) ** 0.25
    D = x_prompt.shape[-1]
    C = conv_w.shape[-1]
    n_att = len(ATT_GROUPS)
    GW = (w_in.shape[-1] - 2 * C) // (3 * n_att)
    H = GW // HEAD_DIM
    n_groups = w_router_group.shape[-1]
    n_experts = w_router_expert.shape[-1]
    F = w_gate.shape[-1]
    row = lambda v: v.reshape(1, -1).astype(F32)

    w_in_b = w_in[0].astype(BF16)
    b_in_r = row(b_in[0])
    w_sections = [(w_in_b[:, :2 * C], b_in_r[:, :2 * C])]
    off = 2 * C
    for _ in range(n_att):
        w_sections.append((w_in_b[:, off:off + 3 * GW], b_in_r[:, off:off + 3 * GW]))
        off += 3 * GW
    w_out_b = w_out[0].astype(BF16)
    wu_out, wo_out = w_out_b[:C], w_out_b[C:]
    RL = LANES
    w_r = jnp.concatenate([w_router_group[0], w_router_expert[0]], axis=1).astype(F32)
    w_r = jnp.pad(w_r, ((0, 0), (0, RL - w_r.shape[1])))
    wrh = w_r.astype(BF16)
    wrl = (w_r - wrh.astype(F32)).astype(BF16)
    b_r = jnp.pad(jnp.concatenate([b_router_group[0], b_router_expert[0]]).astype(F32),
                  (0, RL - n_groups - n_experts)).reshape(1, RL)
    n_heads = n_att * H
    slopes = jnp.exp2(-8.0 * jnp.arange(1, n_heads + 1, dtype=F32) / n_heads).reshape(n_att, H)

    trunks = [x_prompt, x_sample]
    total = sum(x.shape[0] * x.shape[1] for x in trunks)
    staged = []
    offsets = []
    row_offset = 0
    for x in trunks:
        B, S, _ = x.shape
        T = B * S
        t = _tiles(D, S, C, GW, F)
        x2d = x.reshape(T, D)
        xn = _ln_cast(x2d, row(ln_in_g), row(ln_in_b), t['ln_rows'])
        ag = _proj_nat(xn, w_sections[0][0], w_sections[0][1], t['proj_rows'], min(t['proj_cols'], 2 * C))
        u = _conv_module(ag, conv_w[0].astype(F32), row(conv_b[0]), row(conv_ln_g[0]), row(conv_ln_b[0]),
                         B, S, t['conv_rows'])
        qkvs = [_proj_heads(xn, w_sections[1 + g][0], w_sections[1 + g][1], B, S, dil, GW,
                            t['proj_rows'], t['proj_cols'])
                for g, (_, dil) in enumerate(ATT_GROUPS)]
        o = _attention(qkvs, slopes, B, S, H, t['attn_rows'])
        staged.append((x2d, u, o))
        offsets.append((row_offset, T, t))
        row_offset += T
    t0 = offsets[0][2]
    x1, logits = _outproj(staged, row(ln_in_g), row(ln_in_b), wu_out, wo_out, row(b_out[0]),
                          row(ln1_g[0]), row(ln1_b[0]), wrh, wrl, b_r, alpha, t0['out_rows'], t0['out_cols'])

    tm = offsets[0][2]['moe_rows']
    gates, dest, buf_tok, tile_e, tile_valid = _route(logits, n_groups, n_experts, tm)
    nt = tile_e.shape[0]
    xs = _dispatch(buf_tok.reshape(nt, 1, tm), x1)
    hmid = _moe_up(tile_e, tile_valid, xs, w_gate[0], w_up[0], tm, min(t0['moe_cols'], F))
    ys = _moe_down(tile_e, tile_valid, hmid, w_down[0], tm)
    dest3 = dest.reshape(total // tm, tm, TOP_K).transpose(0, 2, 1).reshape(total // tm, 1, TOP_K * tm)
    outs = []
    for x, (ro, T, t) in zip(trunks, offsets):
        y = _combine(dest3, x1, ro, T, gates, row(ln2_g[0]), row(ln2_b[0]), ys, alpha, tm)
        outs.append(y.reshape(x.shape))
    return tuple(outs)
```

```python
import functools

import jax
import jax.numpy as jnp
from jax import lax
from jax.experimental import pallas as pl
from jax.experimental.pallas import tpu as pltpu

F32 = jnp.float32
BF16 = jnp.bfloat16

HEAD_DIM = 128
ATT_GROUPS = ((128, 1), (512, 4), (2048, 16))
HALF_WINDOW = 64
CONV_WIDTH = 31
CONV_HALO = 16
TOP_K = 2
LN_EPS = 1e-5
NEG_INF = -1e30
LANES = 128
V7X_VMEM_BYTES = 64 * 1024 * 1024


def _vmem_limit(nbytes):
    return int(min(max(nbytes * 5 // 4 + (4 << 20), 16 << 20), V7X_VMEM_BYTES - (6 << 20)))


def _ln(x, g, b):
    mu = jnp.mean(x, axis=-1, keepdims=True)
    xc = x - mu
    var = jnp.mean(xc * xc, axis=-1, keepdims=True)
    return xc * lax.rsqrt(var + LN_EPS) * g + b


def _ln_cast_kernel(x_ref, g_ref, b_ref, o_ref):
    o_ref[...] = _ln(x_ref[...], g_ref[...], b_ref[...]).astype(o_ref.dtype)


def _ln_cast(x2d, g, b, tm):
    T, D = x2d.shape
    return pl.pallas_call(
        _ln_cast_kernel, name="ln_cast",
        grid=(T // tm,),
        in_specs=[pl.BlockSpec((tm, D), lambda i: (i, 0)),
                  pl.BlockSpec((1, D), lambda i: (0, 0)),
                  pl.BlockSpec((1, D), lambda i: (0, 0))],
        out_specs=pl.BlockSpec((tm, D), lambda i: (i, 0)),
        out_shape=jax.ShapeDtypeStruct((T, D), BF16),
        compiler_params=pltpu.CompilerParams(
            dimension_semantics=("parallel",),
            vmem_limit_bytes=_vmem_limit(2 * tm * D * 6)),
    )(x2d, g, b)


def _proj_nat_kernel(x_ref, w_ref, b_ref, o_ref):
    acc = jnp.dot(x_ref[...], w_ref[...], preferred_element_type=F32) + b_ref[...]
    o_ref[...] = acc.astype(o_ref.dtype)


def _proj_nat(xn, w, b, tm, tn):
    T, D = xn.shape
    N = w.shape[1]
    return pl.pallas_call(
        _proj_nat_kernel, name="proj_conv",
        grid=(T // tm, N // tn),
        in_specs=[pl.BlockSpec((tm, D), lambda m, n: (m, 0)),
                  pl.BlockSpec((D, tn), lambda m, n: (0, n)),
                  pl.BlockSpec((1, tn), lambda m, n: (0, n))],
        out_specs=pl.BlockSpec((tm, tn), lambda m, n: (m, n)),
        out_shape=jax.ShapeDtypeStruct((T, N), BF16),
        compiler_params=pltpu.CompilerParams(
            dimension_semantics=("parallel", "arbitrary"),
            vmem_limit_bytes=_vmem_limit(2 * (tm * D * 2 + D * tn * 2 + tm * tn * 2) + tm * tn * 4)),
    )(xn, w, b)


def _proj_heads_kernel(x_ref, w_ref, b_ref, o_ref, acc_ref, *, dil, q_blocks, q_scale):
    n = pl.program_id(2)
    nj, tm, _ = acc_ref.shape
    tn = nj * LANES
    scale = jnp.where(n < q_blocks, q_scale, 1.0).astype(F32)
    acc = (jnp.dot(x_ref[...], w_ref[...], preferred_element_type=F32) + b_ref[...]) * scale
    if dil == 1:
        for j in range(tn // LANES):
            o_ref[j, 0, 0] = acc[:, j * LANES:(j + 1) * LANES].astype(o_ref.dtype)
    else:
        for j in range(nj):
            acc_ref[j] = acc[:, j * LANES:(j + 1) * LANES]
        for j in range(nj):
            for r in range(dil):
                o_ref[j, 0, r] = acc_ref[j, pl.ds(r, tm // dil, stride=dil), :].astype(o_ref.dtype)


def _proj_heads(xn, w, b, B, S, dil, group_width, tm, tn):
    T, D = xn.shape
    N = w.shape[1]
    L = S // dil
    mt = S // tm
    kern = functools.partial(_proj_heads_kernel, dil=dil, q_blocks=group_width // tn,
                             q_scale=HEAD_DIM ** -0.5)
    return pl.pallas_call(
        kern, name=f"proj_attn_dil{dil}",
        grid=(B, mt, N // tn),
        in_specs=[pl.BlockSpec((tm, D), lambda bb, m, n: (bb * mt + m, 0)),
                  pl.BlockSpec((D, tn), lambda bb, m, n: (0, n)),
                  pl.BlockSpec((1, tn), lambda bb, m, n: (0, n))],
        out_specs=pl.BlockSpec((tn // LANES, 1, dil, tm // dil, LANES),
                               lambda bb, m, n: (n, bb, 0, m, 0)),
        out_shape=jax.ShapeDtypeStruct((N // LANES, B, dil, L, LANES), BF16),
        scratch_shapes=[pltpu.VMEM((tn // LANES, tm, LANES), F32)],
        compiler_params=pltpu.CompilerParams(
            dimension_semantics=("parallel", "parallel", "arbitrary"),
            vmem_limit_bytes=_vmem_limit(2 * (tm * D * 2 + D * tn * 2 + tm * tn * 2) + 2 * tm * tn * 4)),
    )(xn, w, b)


def _conv_kernel(cur_ref, prev_ref, next_ref, w_ref, cb_ref, g_ref, b_ref, o_ref, ext_ref, y_ref, *, C):
    i = pl.program_id(1)
    last = pl.num_programs(1) - 1
    tp = cur_ref.shape[1]

    def glu(blk):
        a = blk[:, :C].astype(F32)
        gate = blk[:, C:].astype(F32)
        return a * (1.0 / (1.0 + jnp.exp(-gate)))

    ext_ref[0:CONV_HALO] = glu(prev_ref[0]) * jnp.where(i > 0, 1.0, 0.0)
    ext_ref[CONV_HALO:CONV_HALO + tp] = glu(cur_ref[0])
    ext_ref[CONV_HALO + tp:] = glu(next_ref[0]) * jnp.where(i < last, 1.0, 0.0)

    first = CONV_HALO - CONV_WIDTH // 2

    def chan_block(cb, carry):
        c0 = pl.multiple_of(cb * LANES, LANES)
        acc = jnp.zeros((tp, LANES), F32)
        for d in range(CONV_WIDTH):
            acc = acc + ext_ref[pl.ds(first + d, tp), pl.ds(c0, LANES)] * w_ref[pl.ds(d, 1), pl.ds(c0, LANES)]
        y_ref[:, pl.ds(c0, LANES)] = acc + cb_ref[:, pl.ds(c0, LANES)]
        return carry

    lax.fori_loop(0, C // LANES, chan_block, 0)
    y = _ln(y_ref[...], g_ref[...], b_ref[...])
    o_ref[0] = (y * (1.0 / (1.0 + jnp.exp(-y)))).astype(o_ref.dtype)


def _conv_module(ag, conv_w, conv_b, ln_g, ln_b, B, S, tp):
    C = ag.shape[1] // 2
    ag3 = ag.reshape(B, S, 2 * C)
    hb = tp // CONV_HALO
    nh = S // CONV_HALO
    out = pl.pallas_call(
        functools.partial(_conv_kernel, C=C), name="conv_module",
        grid=(B, S // tp),
        in_specs=[pl.BlockSpec((1, tp, 2 * C), lambda b, i: (b, i, 0)),
                  pl.BlockSpec((1, CONV_HALO, 2 * C), lambda b, i: (b, jnp.maximum(i * hb - 1, 0), 0)),
                  pl.BlockSpec((1, CONV_HALO, 2 * C), lambda b, i: (b, jnp.minimum((i + 1) * hb, nh - 1), 0)),
                  pl.BlockSpec((CONV_WIDTH, C), lambda b, i: (0, 0)),
                  pl.BlockSpec((1, C), lambda b, i: (0, 0)),
                  pl.BlockSpec((1, C), lambda b, i: (0, 0)),
                  pl.BlockSpec((1, C), lambda b, i: (0, 0))],
        out_specs=pl.BlockSpec((1, tp, C), lambda b, i: (b, i, 0)),
        out_shape=jax.ShapeDtypeStruct((B, S, C), BF16),
        scratch_shapes=[pltpu.VMEM((tp + 2 * CONV_HALO, C), F32), pltpu.VMEM((tp, C), F32)],
        compiler_params=pltpu.CompilerParams(
            dimension_semantics=("parallel", "parallel"),
            vmem_limit_bytes=_vmem_limit(2 * tp * 2 * C * 2 + 2 * tp * C * 2 + 3 * tp * C * 4)),
    )(ag3, ag3, ag3, conv_w, conv_b, ln_g, ln_b)
    return out.reshape(B * S, C)


def _attn_kernel(slopes_ref, *refs, tp, dils, heads):
    ng = len(dils)
    ins = refs[:7 * ng]
    o_ref = refs[7 * ng]
    scr = refs[7 * ng + 1:]
    kwins, vwins = scr[:ng], scr[ng:2 * ng]
    o_nat, m_nat, l_nat = scr[2 * ng:2 * ng + 3]

    h = pl.program_id(1)
    i = pl.program_id(2)
    W = HALF_WINDOW

    for g, dil in enumerate(dils):
        q_ref, kc, kp, kn, vc, vp, vn = ins[7 * g:7 * g + 7]
        kwin, vwin = kwins[g], vwins[g]
        n = tp // dil
        L = n * pl.num_programs(2)
        sq = min(128, n)
        nsub = n // sq
        for win, cur, prev, nxt in ((kwin, kc, kp, kn), (vwin, vc, vp, vn)):
            win[:, 0:W] = prev[0, 0]
            win[:, W:W + n] = cur[0, 0]
            win[:, W + n:] = nxt[0, 0]

        slope = slopes_ref[g, h] * float(dil)
        row = lax.broadcasted_iota(jnp.int32, (sq, sq + 2 * W), 0)
        col = lax.broadcasted_iota(jnp.int32, (sq, sq + 2 * W), 1)
        dist = jnp.abs(col - W - row)
        bias = jnp.where(dist <= W, -slope * dist.astype(F32), NEG_INF)
        col1 = lax.broadcasted_iota(jnp.int32, (1, sq + 2 * W), 1)

        for r in range(dil):
            for sub in range(nsub):
                qs = q_ref[0, 0, r, sub * sq:(sub + 1) * sq, :]
                kw = kwin[r, sub * sq:sub * sq + sq + 2 * W, :]
                vw = vwin[r, sub * sq:sub * sq + sq + 2 * W, :]
                s = lax.dot_general(qs, kw, (((1,), (1,)), ((), ())), preferred_element_type=F32) + bias
                if sub == 0 or sub == nsub - 1:
                    kidx = i * n + (sub * sq - W) + col1
                    s = s + jnp.where((kidx >= 0) & (kidx < L), 0.0, NEG_INF)
                m = jnp.max(s, axis=-1, keepdims=True)
                p = jnp.exp(s - m)
                l = jnp.sum(p, axis=-1, keepdims=True)
                o = jnp.dot(p.astype(BF16), vw, preferred_element_type=F32)
                if dil == 1:
                    rows = pl.ds(sub * sq, sq)
                else:
                    rows = pl.ds(sub * sq * dil + r, sq, stride=dil)
                o_nat[g, rows, :] = o
                m_nat[g, rows, :] = jnp.broadcast_to(m, (sq, LANES))
                l_nat[g, rows, :] = jnp.broadcast_to(l, (sq, LANES))

    ch = min(128, tp)

    def merge(c, carry):
        r0 = pl.multiple_of(c * ch, ch)
        ms = [m_nat[g, pl.ds(r0, ch), :] for g in range(ng)]
        mx = functools.reduce(jnp.maximum, ms)
        num = jnp.zeros((ch, LANES), F32)
        den = jnp.zeros((ch, LANES), F32)
        for g in range(ng):
            wg = jnp.exp(ms[g] - mx)
            num = num + wg * o_nat[g, pl.ds(r0, ch), :]
            den = den + wg * l_nat[g, pl.ds(r0, ch), :]
        o_ref[0, pl.ds(r0, ch), :] = (num / den).astype(o_ref.dtype)
        return carry

    lax.fori_loop(0, tp // ch, merge, 0)


def _attention(qkvs, slopes, B, S, heads, tp):
    dils = tuple(d for _, d in ATT_GROUPS)
    W = HALF_WINDOW
    H = heads
    in_specs = [pl.BlockSpec(memory_space=pltpu.SMEM)]
    args = [slopes]
    scratch_k, scratch_v = [], []
    nbytes = 0
    for g, dil in enumerate(dils):
        n = tp // dil
        L = S // dil
        hb = n // W
        nh = L // W

        def cur_map(which):
            return lambda b, h, i: (which * H + h, b, 0, i, 0)

        def prev_map(which, hb=hb):
            return lambda b, h, i: (which * H + h, b, 0, jnp.maximum(i * hb - 1, 0), 0)

        def next_map(which, hb=hb, nh=nh):
            return lambda b, h, i: (which * H + h, b, 0, jnp.minimum((i + 1) * hb, nh - 1), 0)

        blk = (1, 1, dil, n, LANES)
        halo = (1, 1, dil, W, LANES)
        in_specs += [pl.BlockSpec(blk, cur_map(0)),
                     pl.BlockSpec(blk, cur_map(1)), pl.BlockSpec(halo, prev_map(1)), pl.BlockSpec(halo, next_map(1)),
                     pl.BlockSpec(blk, cur_map(2)), pl.BlockSpec(halo, prev_map(2)), pl.BlockSpec(halo, next_map(2))]
        args += [qkvs[g]] * 7
        scratch_k.append(pltpu.VMEM((dil, n + 2 * W, LANES), BF16))
        scratch_v.append(pltpu.VMEM((dil, n + 2 * W, LANES), BF16))
        nbytes += 2 * (3 * tp + 4 * dil * W) * LANES * 2 + 2 * (tp + 2 * W * dil) * LANES * 2
    nbytes += 9 * tp * LANES * 4 + 2 * tp * LANES * 2
    ng = len(dils)
    out = pl.pallas_call(
        functools.partial(_attn_kernel, tp=tp, dils=dils, heads=H), name="attention",
        grid=(B, H, S // tp),
        in_specs=in_specs,
        out_specs=pl.BlockSpec((1, tp, LANES), lambda b, h, i: (b, i, h)),
        out_shape=jax.ShapeDtypeStruct((B, S, H * LANES), BF16),
        scratch_shapes=scratch_k + scratch_v + [pltpu.VMEM((ng, tp, LANES), F32)] * 3,
        compiler_params=pltpu.CompilerParams(
            dimension_semantics=("parallel", "parallel", "parallel"),
            vmem_limit_bytes=_vmem_limit(nbytes)),
    )(*args)
    return out.reshape(B * S, H * LANES)


def _ln_rows(v, g_ref, b_ref):
    mu = jnp.mean(v, axis=-1, keepdims=True)
    c = v - mu
    var = jnp.mean(c * c, axis=-1, keepdims=True)
    return c * lax.rsqrt(var + LN_EPS) * g_ref[...] + b_ref[...]


def _outproj_kernel(*refs, alpha, tile_starts):
    nk = len(tile_starts) - 1
    trunk_refs = [refs[3 * k:3 * k + 3] for k in range(nk)]
    (lig_ref, lib_ref, wu_ref, wo_ref, bo_ref, g1_ref, b1_ref,
     x1_ref, acc_ref, us_ref, os_ref) = refs[3 * nk:]
    m = pl.program_id(0)
    n = pl.program_id(1)
    tm = acc_ref.shape[1]
    tn = wu_ref.shape[1]
    chunk = tm // (acc_ref.shape[2] // tn)
    sub = 8
    s = m % 2

    @pl.when((m == 0) & (n == 0))
    def _():
        acc_ref[...] = jnp.zeros_like(acc_ref)
        us_ref[...] = jnp.zeros_like(us_ref)
        os_ref[...] = jnp.zeros_like(os_ref)

    def body(x_ref, u_ref, o_ref):
        r0 = pl.multiple_of(n * chunk, chunk)
        for g in range(chunk // sub):
            rows = pl.ds(r0 + g * sub, sub)
            x1_ref[rows, :] = _ln_rows(acc_ref[s, rows, :], g1_ref, b1_ref)
            acc_ref[s, rows, :] = alpha * _ln_rows(x_ref[rows, :], lig_ref, lib_ref)
        us_ref[s, pl.ds(r0, chunk), :] = u_ref[pl.ds(r0, chunk), :]
        os_ref[s, pl.ds(r0, chunk), :] = o_ref[pl.ds(r0, chunk), :]
        c0 = pl.multiple_of(n * tn, LANES)
        mix = (jnp.dot(us_ref[1 - s], wu_ref[...], preferred_element_type=F32)
               + jnp.dot(os_ref[1 - s], wo_ref[...], preferred_element_type=F32) + bo_ref[...])
        acc_ref[1 - s, :, pl.ds(c0, tn)] = acc_ref[1 - s, :, pl.ds(c0, tn)] + mix

    for k, (x_ref, u_ref, o_ref) in enumerate(trunk_refs):
        lo = tile_starts[k]
        cond = (m >= lo) if k == nk - 1 else ((m >= lo) & (m < tile_starts[k + 1]))
        pl.when(cond)(functools.partial(body, x_ref, u_ref, o_ref))


def _outproj(trunk_inputs, ln_in_g, ln_in_b, wu, wo, b_out, ln1_g, ln1_b, alpha, tm, tn):
    D = trunk_inputs[0][0].shape[1]
    C = wu.shape[0]
    GW = wo.shape[0]
    tile_starts = [0]
    for x, _, _ in trunk_inputs:
        tile_starts.append(tile_starts[-1] + x.shape[0] // tm)
    nm = tile_starts[-1]
    const = lambda m, n: (0, 0)
    in_specs, args = [], []
    for k, (x, u, o) in enumerate(trunk_inputs):
        def rows(m, n, lo=tile_starts[k], cnt=tile_starts[k + 1] - tile_starts[k]):
            return (jnp.clip(m - lo, 0, cnt - 1), 0)
        in_specs += [pl.BlockSpec((tm, D), rows), pl.BlockSpec((tm, C), rows), pl.BlockSpec((tm, GW), rows)]
        args += [x, u, o]
    in_specs += [pl.BlockSpec((1, D), const), pl.BlockSpec((1, D), const),
                 pl.BlockSpec((C, tn), lambda m, n: (0, n)),
                 pl.BlockSpec((GW, tn), lambda m, n: (0, n)),
                 pl.BlockSpec((1, tn), lambda m, n: (0, n)),
                 pl.BlockSpec((1, D), const), pl.BlockSpec((1, D), const)]
    args += [ln_in_g, ln_in_b, wu, wo, b_out, ln1_g, ln1_b]
    nk = len(trunk_inputs)
    nbytes = (2 * (nk * (tm * D * 4 + tm * (C + GW) * 2) + (C + GW) * tn * 2 + tm * D * 4)
              + 2 * tm * D * 4 + 2 * tm * (C + GW) * 2)
    return pl.pallas_call(
        functools.partial(_outproj_kernel, alpha=alpha, tile_starts=tuple(tile_starts)), name="outproj",
        grid=(nm + 2, D // tn),
        in_specs=in_specs,
        out_specs=pl.BlockSpec((tm, D), lambda m, n: (jnp.clip(m - 2, 0, nm - 1), 0)),
        out_shape=jax.ShapeDtypeStruct((nm * tm, D), F32),
        scratch_shapes=[pltpu.VMEM((2, tm, D), F32), pltpu.VMEM((2, tm, C), BF16), pltpu.VMEM((2, tm, GW), BF16)],
        compiler_params=pltpu.CompilerParams(
            dimension_semantics=("arbitrary", "arbitrary"),
            vmem_limit_bytes=_vmem_limit(nbytes)),
    )(*args)


def _router_kernel(x_ref, wh_ref, wl_ref, b_ref, gate_ref, idx_ref, *, n_groups, per):
    x = x_ref[...]
    xh = x.astype(BF16)
    xl = (x - xh.astype(F32)).astype(BF16)
    wh = wh_ref[...]
    lg = (jnp.dot(xh, wh, preferred_element_type=F32) + jnp.dot(xl, wh, preferred_element_type=F32)
          + jnp.dot(xh, wl_ref[...], preferred_element_type=F32) + b_ref[...])
    lane = lax.broadcasted_iota(jnp.int32, lg.shape, 1)
    big = jnp.int32(lg.shape[1])

    def top1(v):
        mx = jnp.max(v, axis=-1, keepdims=True)
        return mx, jnp.min(jnp.where(v == mx, lane, big), axis=-1, keepdims=True)

    gl = jnp.where(lane < n_groups, lg, NEG_INF)
    gmax, g_star = top1(gl)
    p_group = 1.0 / jnp.sum(jnp.exp(gl - gmax), axis=-1, keepdims=True)
    first = n_groups + g_star * per
    el = jnp.where((lane >= first) & (lane < first + per), lg, NEG_INF)
    v0, i0 = top1(el)
    v1, i1 = top1(jnp.where(lane == i0, NEG_INF, el))
    e1 = jnp.exp(v1 - v0)
    g0 = p_group / (1.0 + e1)
    g1 = g0 * e1
    gate_ref[...] = jnp.where(lane == 0, g0, jnp.where(lane == 1, g1, 0.0))
    idx_ref[...] = jnp.where(lane == 0, i0 - n_groups, jnp.where(lane == 1, i1 - n_groups, 0))


def _router(x1, wrh, wrl, br, n_groups, per, tm):
    T, D = x1.shape
    RL = wrh.shape[1]
    const = lambda i: (0, 0)
    return pl.pallas_call(
        functools.partial(_router_kernel, n_groups=n_groups, per=per), name="router",
        grid=(T // tm,),
        in_specs=[pl.BlockSpec((tm, D), lambda i: (i, 0)),
                  pl.BlockSpec((D, RL), const), pl.BlockSpec((D, RL), const), pl.BlockSpec((1, RL), const)],
        out_specs=[pl.BlockSpec((tm, RL), lambda i: (i, 0)), pl.BlockSpec((tm, RL), lambda i: (i, 0))],
        out_shape=[jax.ShapeDtypeStruct((T, RL), F32), jax.ShapeDtypeStruct((T, RL), jnp.int32)],
        compiler_params=pltpu.CompilerParams(
            dimension_semantics=("parallel",),
            vmem_limit_bytes=_vmem_limit(2 * tm * D * 4 + tm * D * 4 + 4 * D * RL * 2 + 4 * tm * RL * 4)),
    )(x1, wrh, wrl, br)


def _start_row_gather(src_hbm, idx_ref, dst, sem, n_rows):
    for r in range(n_rows):
        pltpu.make_async_copy(src_hbm.at[pl.ds(idx_ref[0, 0, r], 1)], dst.at[pl.ds(r, 1)], sem).start()


def _wait_row_gather(src_hbm, dst, sem, n_rows):
    for r in range(n_rows):
        pltpu.make_async_copy(src_hbm.at[pl.ds(0, 1)], dst.at[pl.ds(r, 1)], sem).wait()


def _expert_changed(te_ref, t):
    return (t == 0) | (te_ref[t] != te_ref[jnp.maximum(t - 1, 0)])


def _moe_up_kernel(te_ref, tv_ref, idx_ref, idxn_ref, x_hbm, wg_ref, wu_ref, h_ref, xbuf, sem):
    t = pl.program_id(0)
    nt = pl.num_programs(0)
    tm = xbuf.shape[1]
    slot = t % 2

    @pl.when((t == 0) & (tv_ref[0] > 0))
    def _():
        _start_row_gather(x_hbm, idx_ref, xbuf.at[0], sem.at[0], tm)

    nxt = jnp.minimum(t + 1, nt - 1)

    @pl.when((t + 1 < nt) & (tv_ref[nxt] > 0))
    def _():
        _start_row_gather(x_hbm, idxn_ref, xbuf.at[1 - slot], sem.at[1 - slot], tm)

    @pl.when(tv_ref[t] > 0)
    def _():
        _wait_row_gather(x_hbm, xbuf.at[slot], sem.at[slot], tm)
        xb = xbuf[slot].astype(BF16)
        a = jnp.dot(xb, wg_ref[0], preferred_element_type=F32)
        b = jnp.dot(xb, wu_ref[0], preferred_element_type=F32)
        h_ref[...] = (a * (1.0 / (1.0 + jnp.exp(-a))) * b).astype(h_ref.dtype)

    @pl.when(tv_ref[t] == 0)
    def _():
        h_ref[...] = jnp.zeros_like(h_ref)


def _moe_up(tile_e, tile_valid, buf_tok3, x1, wg, wu, tm):
    nt = tile_e.shape[0]
    E, D, F = wg.shape
    grid_spec = pltpu.PrefetchScalarGridSpec(
        num_scalar_prefetch=2,
        grid=(nt,),
        in_specs=[pl.BlockSpec((1, 1, tm), lambda t, te, tv: (t, 0, 0), memory_space=pltpu.SMEM),
                  pl.BlockSpec((1, 1, tm), lambda t, te, tv: (jnp.minimum(t + 1, nt - 1), 0, 0),
                               memory_space=pltpu.SMEM),
                  pl.BlockSpec(memory_space=pl.ANY),
                  pl.BlockSpec((1, D, F), lambda t, te, tv: (te[t], 0, 0)),
                  pl.BlockSpec((1, D, F), lambda t, te, tv: (te[t], 0, 0))],
        out_specs=pl.BlockSpec((tm, F), lambda t, te, tv: (t, 0)),
        scratch_shapes=[pltpu.VMEM((2, tm, D), F32), pltpu.SemaphoreType.DMA((2,))])
    return pl.pallas_call(
        _moe_up_kernel, name="moe_up",
        grid_spec=grid_spec,
        out_shape=jax.ShapeDtypeStruct((nt * tm, F), BF16),
        compiler_params=pltpu.CompilerParams(
            dimension_semantics=("arbitrary",),
            vmem_limit_bytes=_vmem_limit(4 * D * F * 2 + 2 * tm * D * 4 + 2 * tm * F * 2 + tm * D * 2)),
    )(tile_e, tile_valid, buf_tok3, buf_tok3, x1, wg, wu)


def _bf16_bits(v):
    u = pltpu.bitcast(v, jnp.uint32)
    return u + jnp.uint32(0x7FFF) + ((u >> 16) & jnp.uint32(1))


def _moe_down_kernel(te_ref, tv_ref, h_ref, wd_ref, y_ref, wdb_ref):
    t = pl.program_id(0)
    valid = tv_ref[t] > 0
    half = y_ref.shape[1]

    @pl.when(valid & _expert_changed(te_ref, t))
    def _():
        wdb_ref[...] = wd_ref[0].astype(BF16)

    @pl.when(valid)
    def _():
        y = jnp.dot(h_ref[...], wdb_ref[...], preferred_element_type=F32)
        lo = _bf16_bits(y[:, :half]) >> 16
        hi = _bf16_bits(y[:, half:]) & jnp.uint32(0xFFFF0000)
        y_ref[...] = lo | hi

    @pl.when(jnp.logical_not(valid))
    def _():
        y_ref[...] = jnp.zeros_like(y_ref)


def _moe_down(tile_e, tile_valid, hmid, wd, tm):
    nt = tile_e.shape[0]
    E, F, D = wd.shape
    grid_spec = pltpu.PrefetchScalarGridSpec(
        num_scalar_prefetch=2,
        grid=(nt,),
        in_specs=[pl.BlockSpec((tm, F), lambda t, te, tv: (t, 0)),
                  pl.BlockSpec((1, F, D), lambda t, te, tv: (te[t], 0, 0))],
        out_specs=pl.BlockSpec((tm, D // 2), lambda t, te, tv: (t, 0)),
        scratch_shapes=[pltpu.VMEM((F, D), BF16)])
    return pl.pallas_call(
        _moe_down_kernel, name="moe_down",
        grid_spec=grid_spec,
        out_shape=jax.ShapeDtypeStruct((nt * tm, D // 2), jnp.uint32),
        compiler_params=pltpu.CompilerParams(
            dimension_semantics=("arbitrary",),
            vmem_limit_bytes=_vmem_limit(2 * (F * D * 4 + tm * F * 2 + tm * D * 2) + F * D * 2 + tm * D * 4)),
    )(tile_e, tile_valid, hmid, wd)


def _combine_kernel(idx_ref, idxn_ref, x1_ref, gate_ref, g_ref, b_ref, ys_hbm, o_ref, ybuf, sem, *, alpha):
    t = pl.program_id(0)
    nt = pl.num_programs(0)
    tm, D = x1_ref.shape
    half = D // 2
    slot = t % 2

    @pl.when(t == 0)
    def _():
        _start_row_gather(ys_hbm, idx_ref, ybuf.at[0], sem.at[0], TOP_K * tm)

    @pl.when(t + 1 < nt)
    def _():
        _start_row_gather(ys_hbm, idxn_ref, ybuf.at[1 - slot], sem.at[1 - slot], TOP_K * tm)

    _wait_row_gather(ys_hbm, ybuf.at[slot], sem.at[slot], TOP_K * tm)
    gates = gate_ref[...]
    w0 = ybuf[slot, 0:tm, :]
    w1 = ybuf[slot, tm:2 * tm, :]
    unpack_lo = lambda w: pltpu.bitcast(w << 16, F32)
    unpack_hi = lambda w: pltpu.bitcast(w & jnp.uint32(0xFFFF0000), F32)
    g0, g1 = gates[:, 0:1], gates[:, 1:2]
    r_lo = alpha * x1_ref[:, :half] + g0 * unpack_lo(w0) + g1 * unpack_lo(w1)
    r_hi = alpha * x1_ref[:, half:] + g0 * unpack_hi(w0) + g1 * unpack_hi(w1)
    mu = (jnp.sum(r_lo, axis=-1, keepdims=True) + jnp.sum(r_hi, axis=-1, keepdims=True)) / D
    c_lo = r_lo - mu
    c_hi = r_hi - mu
    var = (jnp.sum(c_lo * c_lo, axis=-1, keepdims=True) + jnp.sum(c_hi * c_hi, axis=-1, keepdims=True)) / D
    rstd = lax.rsqrt(var + LN_EPS)
    o_ref[:, :half] = c_lo * rstd * g_ref[:, :half] + b_ref[:, :half]
    o_ref[:, half:] = c_hi * rstd * g_ref[:, half:] + b_ref[:, half:]


def _combine(dest3, x1, row_offset, T, gates, ln_g, ln_b, ys, alpha, tm):
    D = x1.shape[1]
    nt = T // tm
    mo = row_offset // tm
    return pl.pallas_call(
        functools.partial(_combine_kernel, alpha=alpha), name="moe_combine",
        grid=(nt,),
        in_specs=[pl.BlockSpec((1, 1, TOP_K * tm), lambda t: (mo + t, 0, 0), memory_space=pltpu.SMEM),
                  pl.BlockSpec((1, 1, TOP_K * tm), lambda t: (mo + jnp.minimum(t + 1, nt - 1), 0, 0),
                               memory_space=pltpu.SMEM),
                  pl.BlockSpec((tm, D), lambda t: (mo + t, 0)),
                  pl.BlockSpec((tm, TOP_K), lambda t: (mo + t, 0)),
                  pl.BlockSpec((1, D), lambda t: (0, 0)),
                  pl.BlockSpec((1, D), lambda t: (0, 0)),
                  pl.BlockSpec(memory_space=pl.ANY)],
        out_specs=pl.BlockSpec((tm, D), lambda t: (t, 0)),
        out_shape=jax.ShapeDtypeStruct((T, D), F32),
        scratch_shapes=[pltpu.VMEM((2, TOP_K * tm, D // 2), jnp.uint32), pltpu.SemaphoreType.DMA((2,))],
        compiler_params=pltpu.CompilerParams(
            dimension_semantics=("arbitrary",),
            vmem_limit_bytes=_vmem_limit(4 * tm * D * 4 + 2 * TOP_K * tm * D * 2 + 4 * tm * D * 4)),
    )(dest3, dest3, x1, gates, ln_g, ln_b, ys)


def _dispatch_plan(flat_e, n_experts, tile):
    A = flat_e.shape[0]
    onehot = (flat_e[:, None] == jnp.arange(n_experts, dtype=jnp.int32)[None, :])
    blk = 256
    oh3 = onehot.astype(BF16).reshape(A // blk, blk, n_experts)
    tri = (jnp.arange(blk)[:, None] > jnp.arange(blk)[None, :]).astype(BF16)
    local = jnp.einsum('ij,bjk->bik', tri, oh3, preferred_element_type=F32)
    blk_tot = jnp.sum(oh3.astype(F32), axis=1)
    blk_off = jnp.cumsum(blk_tot, axis=0) - blk_tot
    excl = (local + blk_off[:, None, :]).reshape(A, n_experts)
    counts = jnp.sum(blk_tot, axis=0).astype(jnp.int32)
    rank = jnp.sum(jnp.where(onehot, excl, 0.0), axis=1).astype(jnp.int32)
    padded = (counts + tile - 1) // tile * tile
    pend = jnp.cumsum(padded)
    pstart = pend - padded
    dest = (pstart[flat_e] + rank).astype(jnp.int32)
    n_tiles = A // tile + n_experts
    flat_tok = jnp.arange(A, dtype=jnp.int32) // TOP_K
    buf_tok = jnp.zeros((n_tiles * tile,), jnp.int32).at[dest].set(flat_tok, unique_indices=True)
    tile_start = jnp.arange(n_tiles, dtype=jnp.int32) * tile
    tile_e = jnp.minimum(jnp.searchsorted(pend, tile_start, side='right'), n_experts - 1).astype(jnp.int32)
    tile_valid = (tile_start < pend[-1]).astype(jnp.int32)
    return dest, buf_tok, tile_e, tile_valid


def _tiles(D, S, C, GW, F):
    max_dil = max(d for _, d in ATT_GROUPS)
    t = {}
    t['ln_rows'] = min(256, S)
    t['proj_rows'] = min(1024, S)
    t['proj_cols'] = min(512, GW)
    t['conv_rows'] = min(256, S)
    t['attn_rows'] = min(2048, S)
    t['out_rows'] = min(256, S)
    t['out_cols'] = min(1024, D)
    t['moe_rows'] = min(256, S)
    t['router_rows'] = min(512, S)
    assert t['proj_rows'] % (16 * max_dil) == 0 and t['attn_rows'] % (HALF_WINDOW * max_dil) == 0
    assert S % t['proj_rows'] == 0 and S % t['attn_rows'] == 0 and S % t['conv_rows'] == 0
    return t


def kernel(x_prompt, x_sample, ln_in_g, ln_in_b, w_in, b_in, conv_w, conv_b, conv_ln_g, conv_ln_b, w_out, b_out,
           ln1_g, ln1_b, w_router_group, b_router_group, w_router_expert, b_router_expert, w_gate, w_up, w_down,
           ln2_g, ln2_b):
    depth = w_in.shape[0]
    assert depth == 1
    assert all(w // (2 * d) == HALF_WINDOW for w, d in ATT_GROUPS)
    alpha = (2.0 * depth) ** 0.25
    D = x_prompt.shape[-1]
    C = conv_w.shape[-1]
    n_att = len(ATT_GROUPS)
    GW = (w_in.shape[-1] - 2 * C) // (3 * n_att)
    H = GW // HEAD_DIM
    n_groups = w_router_group.shape[-1]
    n_experts = w_router_expert.shape[-1]
    F = w_gate.shape[-1]
    row = lambda v: v.reshape(1, -1).astype(F32)

    w_in_b = w_in[0].astype(BF16)
    b_in_r = row(b_in[0])
    w_sections = [(w_in_b[:, :2 * C], b_in_r[:, :2 * C])]
    off = 2 * C
    for _ in range(n_att):
        w_sections.append((w_in_b[:, off:off + 3 * GW], b_in_r[:, off:off + 3 * GW]))
        off += 3 * GW
    w_out_b = w_out[0].astype(BF16)
    wu_out, wo_out = w_out_b[:C], w_out_b[C:]
    RL = LANES
    w_r = jnp.concatenate([w_router_group[0], w_router_expert[0]], axis=1).astype(F32)
    w_r = jnp.pad(w_r, ((0, 0), (0, RL - w_r.shape[1])))
    wrh = w_r.astype(BF16)
    wrl = (w_r - wrh.astype(F32)).astype(BF16)
    b_r = jnp.pad(jnp.concatenate([b_router_group[0], b_router_expert[0]]).astype(F32),
                  (0, RL - n_groups - n_experts)).reshape(1, RL)
    n_heads = n_att * H
    slopes = jnp.exp2(-8.0 * jnp.arange(1, n_heads + 1, dtype=F32) / n_heads).reshape(n_att, H)

    trunks = [x_prompt, x_sample]
    total = sum(x.shape[0] * x.shape[1] for x in trunks)
    staged = []
    offsets = []
    row_offset = 0
    for x in trunks:
        B, S, _ = x.shape
        T = B * S
        t = _tiles(D, S, C, GW, F)
        x2d = x.reshape(T, D)
        xn = _ln_cast(x2d, row(ln_in_g), row(ln_in_b), t['ln_rows'])
        ag = _proj_nat(xn, w_sections[0][0], w_sections[0][1], t['proj_rows'], min(t['proj_cols'], 2 * C))
        u = _conv_module(ag, conv_w[0].astype(F32), row(conv_b[0]), row(conv_ln_g[0]), row(conv_ln_b[0]),
                         B, S, t['conv_rows'])
        qkvs = [_proj_heads(xn, w_sections[1 + g][0], w_sections[1 + g][1], B, S, dil, GW,
                            t['proj_rows'], t['proj_cols'])
                for g, (_, dil) in enumerate(ATT_GROUPS)]
        o = _attention(qkvs, slopes, B, S, H, t['attn_rows'])
        staged.append((x2d, u, o))
        offsets.append((row_offset, T, t))
        row_offset += T
    t0 = offsets[0][2]
    x1 = _outproj(staged, row(ln_in_g), row(ln_in_b), wu_out, wo_out, row(b_out[0]),
                  row(ln1_g[0]), row(ln1_b[0]), alpha, t0['out_rows'], t0['out_cols'])
    gate_l, idx_l = _router(x1, wrh, wrl, b_r, n_groups, n_experts // n_groups, t0['router_rows'])
    gates = gate_l[:, :TOP_K]

    tm = t0['moe_rows']
    dest, buf_tok, tile_e, tile_valid = _dispatch_plan(idx_l[:, :TOP_K].reshape(total * TOP_K), n_experts, tm)
    nt = tile_e.shape[0]
    hmid = _moe_up(tile_e, tile_valid, buf_tok.reshape(nt, 1, tm), x1, w_gate[0].astype(BF16),
                   w_up[0].astype(BF16), tm)
    ys = _moe_down(tile_e, tile_valid, hmid, w_down[0], tm)
    dest3 = dest.reshape(total // tm, tm, TOP_K).transpose(0, 2, 1).reshape(total // tm, 1, TOP_K * tm)
    outs = []
    for x, (ro, T, t) in zip(trunks, offsets):
        y = _combine(dest3, x1, ro, T, gates, row(ln2_g[0]), row(ln2_b[0]), ys, alpha, tm)
        outs.append(y.reshape(x.shape))
    return tuple(outs)
```

```python
import functools

import jax
import jax.numpy as jnp
from jax import lax
from jax.experimental import pallas as pl
from jax.experimental.pallas import tpu as pltpu

F32 = jnp.float32
BF16 = jnp.bfloat16

HEAD_DIM = 128
ATT_GROUPS = ((128, 1), (512, 4), (2048, 16))
HALF_WINDOW = 64
CONV_WIDTH = 31
CONV_HALO = 16
TOP_K = 2
LN_EPS = 1e-5
NEG_INF = -1e30
LANES = 128
V7X_VMEM_BYTES = 64 * 1024 * 1024


def _vmem_limit(nbytes):
    return int(min(max(nbytes * 5 // 4 + (4 << 20), 16 << 20), V7X_VMEM_BYTES - (6 << 20)))


def _ln(x, g, b):
    mu = jnp.mean(x, axis=-1, keepdims=True)
    xc = x - mu
    var = jnp.mean(xc * xc, axis=-1, keepdims=True)
    return xc * lax.rsqrt(var + LN_EPS) * g + b


def _ln_cast_kernel(x_ref, g_ref, b_ref, o_ref):
    o_ref[...] = _ln(x_ref[...], g_ref[...], b_ref[...]).astype(o_ref.dtype)


def _ln_cast(x2d, g, b, tm):
    T, D = x2d.shape
    return pl.pallas_call(
        _ln_cast_kernel, name="ln_cast",
        grid=(T // tm,),
        in_specs=[pl.BlockSpec((tm, D), lambda i: (i, 0)),
                  pl.BlockSpec((1, D), lambda i: (0, 0)),
                  pl.BlockSpec((1, D), lambda i: (0, 0))],
        out_specs=pl.BlockSpec((tm, D), lambda i: (i, 0)),
        out_shape=jax.ShapeDtypeStruct((T, D), BF16),
        compiler_params=pltpu.CompilerParams(
            dimension_semantics=("parallel",),
            vmem_limit_bytes=_vmem_limit(2 * tm * D * 6)),
    )(x2d, g, b)


def _proj_nat_kernel(x_ref, w_ref, b_ref, o_ref):
    acc = jnp.dot(x_ref[...], w_ref[...], preferred_element_type=F32) + b_ref[...]
    o_ref[...] = acc.astype(o_ref.dtype)


def _proj_nat(xn, w, b, tm, tn):
    T, D = xn.shape
    N = w.shape[1]
    return pl.pallas_call(
        _proj_nat_kernel, name="proj_conv",
        grid=(T // tm, N // tn),
        in_specs=[pl.BlockSpec((tm, D), lambda m, n: (m, 0)),
                  pl.BlockSpec((D, tn), lambda m, n: (0, n)),
                  pl.BlockSpec((1, tn), lambda m, n: (0, n))],
        out_specs=pl.BlockSpec((tm, tn), lambda m, n: (m, n)),
        out_shape=jax.ShapeDtypeStruct((T, N), BF16),
        compiler_params=pltpu.CompilerParams(
            dimension_semantics=("parallel", "arbitrary"),
            vmem_limit_bytes=_vmem_limit(2 * (tm * D * 2 + D * tn * 2 + tm * tn * 2) + tm * tn * 4)),
    )(xn, w, b)


def _proj_heads_kernel(x_ref, w_ref, b_ref, o_ref, acc_ref, *, dil, q_blocks, q_scale):
    n = pl.program_id(2)
    nj, tm, _ = acc_ref.shape
    tn = nj * LANES
    scale = jnp.where(n < q_blocks, q_scale, 1.0).astype(F32)
    acc = (jnp.dot(x_ref[...], w_ref[...], preferred_element_type=F32) + b_ref[...]) * scale
    if dil == 1:
        for j in range(tn // LANES):
            o_ref[j, 0, 0] = acc[:, j * LANES:(j + 1) * LANES].astype(o_ref.dtype)
    else:
        for j in range(nj):
            acc_ref[j] = acc[:, j * LANES:(j + 1) * LANES]
        for j in range(nj):
            for r in range(dil):
                o_ref[j, 0, r] = acc_ref[j, pl.ds(r, tm // dil, stride=dil), :].astype(o_ref.dtype)


def _proj_heads(xn, w, b, B, S, dil, group_width, tm, tn):
    T, D = xn.shape
    N = w.shape[1]
    L = S // dil
    mt = S // tm
    kern = functools.partial(_proj_heads_kernel, dil=dil, q_blocks=group_width // tn,
                             q_scale=HEAD_DIM ** -0.5)
    return pl.pallas_call(
        kern, name=f"proj_attn_dil{dil}",
        grid=(B, mt, N // tn),
        in_specs=[pl.BlockSpec((tm, D), lambda bb, m, n: (bb * mt + m, 0)),
                  pl.BlockSpec((D, tn), lambda bb, m, n: (0, n)),
                  pl.BlockSpec((1, tn), lambda bb, m, n: (0, n))],
        out_specs=pl.BlockSpec((tn // LANES, 1, dil, tm // dil, LANES),
                               lambda bb, m, n: (n, bb, 0, m, 0)),
        out_shape=jax.ShapeDtypeStruct((N // LANES, B, dil, L, LANES), BF16),
        scratch_shapes=[pltpu.VMEM((tn // LANES, tm, LANES), F32)],
        compiler_params=pltpu.CompilerParams(
            dimension_semantics=("parallel", "parallel", "arbitrary"),
            vmem_limit_bytes=_vmem_limit(2 * (tm * D * 2 + D * tn * 2 + tm * tn * 2) + 2 * tm * tn * 4)),
    )(xn, w, b)


def _conv_kernel(cur_ref, prev_ref, next_ref, w_ref, cb_ref, g_ref, b_ref, o_ref, ext_ref, y_ref, sh_ref, *, C):
    i = pl.program_id(1)
    last = pl.num_programs(1) - 1
    tp = cur_ref.shape[1]

    def glu(blk):
        a = blk[:, :C].astype(F32)
        gate = blk[:, C:].astype(F32)
        return a * (1.0 / (1.0 + jnp.exp(-gate)))

    ext_ref[0:CONV_HALO] = glu(prev_ref[0]) * jnp.where(i > 0, 1.0, 0.0)
    ext_ref[CONV_HALO:CONV_HALO + tp] = glu(cur_ref[0])
    ext_ref[CONV_HALO + tp:] = glu(next_ref[0]) * jnp.where(i < last, 1.0, 0.0)

    first = CONV_HALO - CONV_WIDTH // 2

    sub = 8
    span = sh_ref.shape[1]

    def chan_block(cb, carry):
        c0 = pl.multiple_of(cb * LANES, LANES)
        for b in range(1, sub):
            sh_ref[b] = ext_ref[pl.ds(b, span), pl.ds(c0, LANES)]
        acc = jnp.zeros((tp, LANES), F32)
        for d in range(CONV_WIDTH):
            q = first + d
            a8, b = (q // sub) * sub, q % sub
            if b == 0:
                win = ext_ref[pl.ds(a8, tp), pl.ds(c0, LANES)]
            else:
                win = sh_ref[b, pl.ds(a8, tp), :]
            acc = acc + win * w_ref[pl.ds(d, 1), pl.ds(c0, LANES)]
        y_ref[:, pl.ds(c0, LANES)] = acc + cb_ref[:, pl.ds(c0, LANES)]
        return carry

    lax.fori_loop(0, C // LANES, chan_block, 0)
    y = _ln(y_ref[...], g_ref[...], b_ref[...])
    o_ref[0] = (y * (1.0 / (1.0 + jnp.exp(-y)))).astype(o_ref.dtype)


def _conv_module(ag, conv_w, conv_b, ln_g, ln_b, B, S, tp):
    C = ag.shape[1] // 2
    ag3 = ag.reshape(B, S, 2 * C)
    hb = tp // CONV_HALO
    nh = S // CONV_HALO
    out = pl.pallas_call(
        functools.partial(_conv_kernel, C=C), name="conv_module",
        grid=(B, S // tp),
        in_specs=[pl.BlockSpec((1, tp, 2 * C), lambda b, i: (b, i, 0)),
                  pl.BlockSpec((1, CONV_HALO, 2 * C), lambda b, i: (b, jnp.maximum(i * hb - 1, 0), 0)),
                  pl.BlockSpec((1, CONV_HALO, 2 * C), lambda b, i: (b, jnp.minimum((i + 1) * hb, nh - 1), 0)),
                  pl.BlockSpec((CONV_WIDTH, C), lambda b, i: (0, 0)),
                  pl.BlockSpec((1, C), lambda b, i: (0, 0)),
                  pl.BlockSpec((1, C), lambda b, i: (0, 0)),
                  pl.BlockSpec((1, C), lambda b, i: (0, 0))],
        out_specs=pl.BlockSpec((1, tp, C), lambda b, i: (b, i, 0)),
        out_shape=jax.ShapeDtypeStruct((B, S, C), BF16),
        scratch_shapes=[pltpu.VMEM((tp + 2 * CONV_HALO, C), F32), pltpu.VMEM((tp, C), F32),
                        pltpu.VMEM((8, tp + 2 * CONV_HALO - 8, LANES), F32)],
        compiler_params=pltpu.CompilerParams(
            dimension_semantics=("parallel", "parallel"),
            vmem_limit_bytes=_vmem_limit(2 * tp * 2 * C * 2 + 2 * tp * C * 2 + 3 * tp * C * 4)),
    )(ag3, ag3, ag3, conv_w, conv_b, ln_g, ln_b)
    return out.reshape(B * S, C)


def _attn_kernel(slopes_ref, *refs, tp, dils, heads):
    ng = len(dils)
    ins = refs[:7 * ng]
    o_ref = refs[7 * ng]
    scr = refs[7 * ng + 1:]
    kwins, vwins = scr[:ng], scr[ng:2 * ng]
    o_nat, m_nat, l_nat = scr[2 * ng:2 * ng + 3]

    h = pl.program_id(1)
    i = pl.program_id(2)
    W = HALF_WINDOW

    for g, dil in enumerate(dils):
        q_ref, kc, kp, kn, vc, vp, vn = ins[7 * g:7 * g + 7]
        kwin, vwin = kwins[g], vwins[g]
        n = tp // dil
        L = n * pl.num_programs(2)
        sq = min(128, n)
        nsub = n // sq
        for win, cur, prev, nxt in ((kwin, kc, kp, kn), (vwin, vc, vp, vn)):
            win[:, 0:W] = prev[0, 0]
            win[:, W:W + n] = cur[0, 0]
            win[:, W + n:] = nxt[0, 0]

        slope = slopes_ref[g, h] * float(dil)
        row = lax.broadcasted_iota(jnp.int32, (sq, sq + 2 * W), 0)
        col = lax.broadcasted_iota(jnp.int32, (sq, sq + 2 * W), 1)
        dist = jnp.abs(col - W - row)
        bias = jnp.where(dist <= W, -slope * dist.astype(F32), NEG_INF)
        col1 = lax.broadcasted_iota(jnp.int32, (1, sq + 2 * W), 1)

        for r in range(dil):
            for sub in range(nsub):
                qs = q_ref[0, 0, r, sub * sq:(sub + 1) * sq, :]
                kw = kwin[r, sub * sq:sub * sq + sq + 2 * W, :]
                vw = vwin[r, sub * sq:sub * sq + sq + 2 * W, :]
                s = lax.dot_general(qs, kw, (((1,), (1,)), ((), ())), preferred_element_type=F32) + bias
                if sub == 0 or sub == nsub - 1:
                    kidx = i * n + (sub * sq - W) + col1
                    s = s + jnp.where((kidx >= 0) & (kidx < L), 0.0, NEG_INF)
                m = jnp.max(s, axis=-1, keepdims=True)
                p = jnp.exp(s - m)
                l = jnp.sum(p, axis=-1, keepdims=True)
                o = jnp.dot(p.astype(BF16), vw, preferred_element_type=F32)
                if dil == 1:
                    rows = pl.ds(sub * sq, sq)
                else:
                    rows = pl.ds(sub * sq * dil + r, sq, stride=dil)
                o_nat[g, rows, :] = o
                m_nat[g, rows, :] = jnp.broadcast_to(m, (sq, LANES))
                l_nat[g, rows, :] = jnp.broadcast_to(l, (sq, LANES))

    ch = min(128, tp)

    def merge(c, carry):
        r0 = pl.multiple_of(c * ch, ch)
        ms = [m_nat[g, pl.ds(r0, ch), :] for g in range(ng)]
        mx = functools.reduce(jnp.maximum, ms)
        num = jnp.zeros((ch, LANES), F32)
        den = jnp.zeros((ch, LANES), F32)
        for g in range(ng):
            wg = jnp.exp(ms[g] - mx)
            num = num + wg * o_nat[g, pl.ds(r0, ch), :]
            den = den + wg * l_nat[g, pl.ds(r0, ch), :]
        o_ref[0, pl.ds(r0, ch), :] = (num / den).astype(o_ref.dtype)
        return carry

    lax.fori_loop(0, tp // ch, merge, 0)


def _attention(qkvs, slopes, B, S, heads, tp):
    dils = tuple(d for _, d in ATT_GROUPS)
    W = HALF_WINDOW
    H = heads
    in_specs = [pl.BlockSpec(memory_space=pltpu.SMEM)]
    args = [slopes]
    scratch_k, scratch_v = [], []
    nbytes = 0
    for g, dil in enumerate(dils):
        n = tp // dil
        L = S // dil
        hb = n // W
        nh = L // W

        def cur_map(which):
            return lambda b, h, i: (which * H + h, b, 0, i, 0)

        def prev_map(which, hb=hb):
            return lambda b, h, i: (which * H + h, b, 0, jnp.maximum(i * hb - 1, 0), 0)

        def next_map(which, hb=hb, nh=nh):
            return lambda b, h, i: (which * H + h, b, 0, jnp.minimum((i + 1) * hb, nh - 1), 0)

        blk = (1, 1, dil, n, LANES)
        halo = (1, 1, dil, W, LANES)
        in_specs += [pl.BlockSpec(blk, cur_map(0)),
                     pl.BlockSpec(blk, cur_map(1)), pl.BlockSpec(halo, prev_map(1)), pl.BlockSpec(halo, next_map(1)),
                     pl.BlockSpec(blk, cur_map(2)), pl.BlockSpec(halo, prev_map(2)), pl.BlockSpec(halo, next_map(2))]
        args += [qkvs[g]] * 7
        scratch_k.append(pltpu.VMEM((dil, n + 2 * W, LANES), BF16))
        scratch_v.append(pltpu.VMEM((dil, n + 2 * W, LANES), BF16))
        nbytes += 2 * (3 * tp + 4 * dil * W) * LANES * 2 + 2 * (tp + 2 * W * dil) * LANES * 2
    nbytes += 9 * tp * LANES * 4 + 2 * tp * LANES * 2
    ng = len(dils)
    out = pl.pallas_call(
        functools.partial(_attn_kernel, tp=tp, dils=dils, heads=H), name="attention",
        grid=(B, H, S // tp),
        in_specs=in_specs,
        out_specs=pl.BlockSpec((1, tp, LANES), lambda b, h, i: (b, i, h)),
        out_shape=jax.ShapeDtypeStruct((B, S, H * LANES), BF16),
        scratch_shapes=scratch_k + scratch_v + [pltpu.VMEM((ng, tp, LANES), F32)] * 3,
        compiler_params=pltpu.CompilerParams(
            dimension_semantics=("parallel", "parallel", "parallel"),
            vmem_limit_bytes=_vmem_limit(nbytes)),
    )(*args)
    return out.reshape(B * S, H * LANES)


def _ln_rows(v, g_ref, b_ref):
    mu = jnp.mean(v, axis=-1, keepdims=True)
    c = v - mu
    var = jnp.mean(c * c, axis=-1, keepdims=True)
    return c * lax.rsqrt(var + LN_EPS) * g_ref[...] + b_ref[...]


def _outproj_kernel(*refs, alpha, tile_starts, tn):
    nk = len(tile_starts) - 1
    trunk_refs = [refs[3 * k:3 * k + 3] for k in range(nk)]
    (lig_ref, lib_ref, wu_ref, wo_ref, bo_ref, g1_ref, b1_ref,
     x1_ref, acc_ref, us_ref, os_ref) = refs[3 * nk:]
    m = pl.program_id(0)
    n = pl.program_id(1)
    tm = acc_ref.shape[1]
    chunk = tm // (acc_ref.shape[2] // tn)
    sub = 8
    s = m % 2

    @pl.when((m == 0) & (n == 0))
    def _():
        acc_ref[...] = jnp.zeros_like(acc_ref)
        us_ref[...] = jnp.zeros_like(us_ref)
        os_ref[...] = jnp.zeros_like(os_ref)

    def body(x_ref, u_ref, o_ref):
        r0 = pl.multiple_of(n * chunk, chunk)
        for g in range(chunk // sub):
            rows = pl.ds(r0 + g * sub, sub)
            x1_ref[rows, :] = _ln_rows(acc_ref[s, rows, :], g1_ref, b1_ref)
            acc_ref[s, rows, :] = alpha * _ln_rows(x_ref[rows, :], lig_ref, lib_ref)
        us_ref[s, pl.ds(r0, chunk), :] = u_ref[pl.ds(r0, chunk), :]
        os_ref[s, pl.ds(r0, chunk), :] = o_ref[pl.ds(r0, chunk), :]
        c0 = pl.multiple_of(n * tn, LANES)
        mix = (jnp.dot(us_ref[1 - s], wu_ref[:, pl.ds(c0, tn)], preferred_element_type=F32)
               + jnp.dot(os_ref[1 - s], wo_ref[:, pl.ds(c0, tn)], preferred_element_type=F32)
               + bo_ref[:, pl.ds(c0, tn)])
        acc_ref[1 - s, :, pl.ds(c0, tn)] = acc_ref[1 - s, :, pl.ds(c0, tn)] + mix

    for k, (x_ref, u_ref, o_ref) in enumerate(trunk_refs):
        lo = tile_starts[k]
        cond = (m >= lo) if k == nk - 1 else ((m >= lo) & (m < tile_starts[k + 1]))
        pl.when(cond)(functools.partial(body, x_ref, u_ref, o_ref))


def _outproj(trunk_inputs, ln_in_g, ln_in_b, wu, wo, b_out, ln1_g, ln1_b, alpha, tm, tn):
    D = trunk_inputs[0][0].shape[1]
    C = wu.shape[0]
    GW = wo.shape[0]
    tile_starts = [0]
    for x, _, _ in trunk_inputs:
        tile_starts.append(tile_starts[-1] + x.shape[0] // tm)
    nm = tile_starts[-1]
    const = lambda m, n: (0, 0)
    in_specs, args = [], []
    for k, (x, u, o) in enumerate(trunk_inputs):
        def rows(m, n, lo=tile_starts[k], cnt=tile_starts[k + 1] - tile_starts[k]):
            return (jnp.clip(m - lo, 0, cnt - 1), 0)
        in_specs += [pl.BlockSpec((tm, D), rows), pl.BlockSpec((tm, C), rows), pl.BlockSpec((tm, GW), rows)]
        args += [x, u, o]
    in_specs += [pl.BlockSpec((1, D), const), pl.BlockSpec((1, D), const),
                 pl.BlockSpec((C, D), const, pipeline_mode=pl.Buffered(1)),
                 pl.BlockSpec((GW, D), const, pipeline_mode=pl.Buffered(1)),
                 pl.BlockSpec((1, D), const),
                 pl.BlockSpec((1, D), const), pl.BlockSpec((1, D), const)]
    args += [ln_in_g, ln_in_b, wu, wo, b_out, ln1_g, ln1_b]
    nk = len(trunk_inputs)
    nbytes = (2 * (nk * (tm * D * 4 + tm * (C + GW) * 2) + tm * D * 4) + (C + GW) * D * 2
              + 2 * tm * D * 4 + 2 * tm * (C + GW) * 2)
    return pl.pallas_call(
        functools.partial(_outproj_kernel, alpha=alpha, tile_starts=tuple(tile_starts), tn=tn), name="outproj",
        grid=(nm + 2, D // tn),
        in_specs=in_specs,
        out_specs=pl.BlockSpec((tm, D), lambda m, n: (jnp.clip(m - 2, 0, nm - 1), 0)),
        out_shape=jax.ShapeDtypeStruct((nm * tm, D), F32),
        scratch_shapes=[pltpu.VMEM((2, tm, D), F32), pltpu.VMEM((2, tm, C), BF16), pltpu.VMEM((2, tm, GW), BF16)],
        compiler_params=pltpu.CompilerParams(
            dimension_semantics=("arbitrary", "arbitrary"),
            vmem_limit_bytes=_vmem_limit(nbytes)),
    )(*args)


def _router_kernel(x_ref, wh_ref, wl_ref, b_ref, gate_ref, idx_ref, *, n_groups, per):
    x = x_ref[...]
    xh = x.astype(BF16)
    xl = (x - xh.astype(F32)).astype(BF16)
    wh = wh_ref[...]
    lg = (jnp.dot(xh, wh, preferred_element_type=F32) + jnp.dot(xl, wh, preferred_element_type=F32)
          + jnp.dot(xh, wl_ref[...], preferred_element_type=F32) + b_ref[...])
    lane = lax.broadcasted_iota(jnp.int32, lg.shape, 1)
    big = jnp.int32(lg.shape[1])

    def top1(v):
        mx = jnp.max(v, axis=-1, keepdims=True)
        return mx, jnp.min(jnp.where(v == mx, lane, big), axis=-1, keepdims=True)

    gl = jnp.where(lane < n_groups, lg, NEG_INF)
    gmax, g_star = top1(gl)
    p_group = 1.0 / jnp.sum(jnp.exp(gl - gmax), axis=-1, keepdims=True)
    first = n_groups + g_star * per
    el = jnp.where((lane >= first) & (lane < first + per), lg, NEG_INF)
    v0, i0 = top1(el)
    v1, i1 = top1(jnp.where(lane == i0, NEG_INF, el))
    e1 = jnp.exp(v1 - v0)
    g0 = p_group / (1.0 + e1)
    g1 = g0 * e1
    gate_ref[...] = jnp.where(lane == 0, g0, jnp.where(lane == 1, g1, 0.0))
    idx_ref[...] = jnp.where(lane == 0, i0 - n_groups, jnp.where(lane == 1, i1 - n_groups, 0))


def _router(x1, wrh, wrl, br, n_groups, per, tm):
    T, D = x1.shape
    RL = wrh.shape[1]
    const = lambda i: (0, 0)
    return pl.pallas_call(
        functools.partial(_router_kernel, n_groups=n_groups, per=per), name="router",
        grid=(T // tm,),
        in_specs=[pl.BlockSpec((tm, D), lambda i: (i, 0)),
                  pl.BlockSpec((D, RL), const), pl.BlockSpec((D, RL), const), pl.BlockSpec((1, RL), const)],
        out_specs=[pl.BlockSpec((tm, RL), lambda i: (i, 0)), pl.BlockSpec((tm, RL), lambda i: (i, 0))],
        out_shape=[jax.ShapeDtypeStruct((T, RL), F32), jax.ShapeDtypeStruct((T, RL), jnp.int32)],
        compiler_params=pltpu.CompilerParams(
            dimension_semantics=("parallel",),
            vmem_limit_bytes=_vmem_limit(2 * tm * D * 4 + tm * D * 4 + 4 * D * RL * 2 + 4 * tm * RL * 4)),
    )(x1, wrh, wrl, br)


def _start_row_gather(src_hbm, idx_ref, dst, sem, n_rows):
    for r in range(n_rows):
        pltpu.make_async_copy(src_hbm.at[pl.ds(idx_ref[0, 0, r], 1)], dst.at[pl.ds(r, 1)], sem).start()


def _wait_row_gather(src_hbm, dst, sem, n_rows):
    for r in range(n_rows):
        pltpu.make_async_copy(src_hbm.at[pl.ds(0, 1)], dst.at[pl.ds(r, 1)], sem).wait()


def _expert_changed(te_ref, t):
    return (t == 0) | (te_ref[t] != te_ref[jnp.maximum(t - 1, 0)])


def _moe_up_kernel(te_ref, tv_ref, idx_ref, idxn_ref, x_hbm, wg_ref, wu_ref, h_ref, xbuf, sem):
    t = pl.program_id(0)
    nt = pl.num_programs(0)
    tm = xbuf.shape[1]
    slot = t % 2

    @pl.when((t == 0) & (tv_ref[0] > 0))
    def _():
        _start_row_gather(x_hbm, idx_ref, xbuf.at[0], sem.at[0], tm)

    nxt = jnp.minimum(t + 1, nt - 1)

    @pl.when((t + 1 < nt) & (tv_ref[nxt] > 0))
    def _():
        _start_row_gather(x_hbm, idxn_ref, xbuf.at[1 - slot], sem.at[1 - slot], tm)

    @pl.when(tv_ref[t] > 0)
    def _():
        _wait_row_gather(x_hbm, xbuf.at[slot], sem.at[slot], tm)
        xb = xbuf[slot].astype(BF16)
        a = jnp.dot(xb, wg_ref[0], preferred_element_type=F32)
        b = jnp.dot(xb, wu_ref[0], preferred_element_type=F32)
        h_ref[...] = (a * (1.0 / (1.0 + jnp.exp(-a))) * b).astype(h_ref.dtype)

    @pl.when(tv_ref[t] == 0)
    def _():
        h_ref[...] = jnp.zeros_like(h_ref)


def _moe_up(tile_e, tile_valid, buf_tok3, x1, wg, wu, tm):
    nt = tile_e.shape[0]
    E, D, F = wg.shape
    grid_spec = pltpu.PrefetchScalarGridSpec(
        num_scalar_prefetch=2,
        grid=(nt,),
        in_specs=[pl.BlockSpec((1, 1, tm), lambda t, te, tv: (t, 0, 0), memory_space=pltpu.SMEM),
                  pl.BlockSpec((1, 1, tm), lambda t, te, tv: (jnp.minimum(t + 1, nt - 1), 0, 0),
                               memory_space=pltpu.SMEM),
                  pl.BlockSpec(memory_space=pl.ANY),
                  pl.BlockSpec((1, D, F), lambda t, te, tv: (te[t], 0, 0)),
                  pl.BlockSpec((1, D, F), lambda t, te, tv: (te[t], 0, 0))],
        out_specs=pl.BlockSpec((tm, F), lambda t, te, tv: (t, 0)),
        scratch_shapes=[pltpu.VMEM((2, tm, D), F32), pltpu.SemaphoreType.DMA((2,))])
    return pl.pallas_call(
        _moe_up_kernel, name="moe_up",
        grid_spec=grid_spec,
        out_shape=jax.ShapeDtypeStruct((nt * tm, F), BF16),
        compiler_params=pltpu.CompilerParams(
            dimension_semantics=("arbitrary",),
            vmem_limit_bytes=_vmem_limit(4 * D * F * 2 + 2 * tm * D * 4 + 2 * tm * F * 2 + tm * D * 2)),
    )(tile_e, tile_valid, buf_tok3, buf_tok3, x1, wg, wu)


def _bf16_bits(v):
    u = pltpu.bitcast(v, jnp.uint32)
    return u + jnp.uint32(0x7FFF) + ((u >> 16) & jnp.uint32(1))


def _moe_down_kernel(te_ref, tv_ref, h_ref, wd_ref, y_ref, wdb_ref):
    t = pl.program_id(0)
    valid = tv_ref[t] > 0
    half = y_ref.shape[1]

    @pl.when(valid & _expert_changed(te_ref, t))
    def _():
        wdb_ref[...] = wd_ref[0].astype(BF16)

    @pl.when(valid)
    def _():
        y = jnp.dot(h_ref[...], wdb_ref[...], preferred_element_type=F32)
        lo = _bf16_bits(y[:, :half]) >> 16
        hi = _bf16_bits(y[:, half:]) & jnp.uint32(0xFFFF0000)
        y_ref[...] = lo | hi

    @pl.when(jnp.logical_not(valid))
    def _():
        y_ref[...] = jnp.zeros_like(y_ref)


def _moe_down(tile_e, tile_valid, hmid, wd, tm):
    nt = tile_e.shape[0]
    E, F, D = wd.shape
    grid_spec = pltpu.PrefetchScalarGridSpec(
        num_scalar_prefetch=2,
        grid=(nt,),
        in_specs=[pl.BlockSpec((tm, F), lambda t, te, tv: (t, 0)),
                  pl.BlockSpec((1, F, D), lambda t, te, tv: (te[t], 0, 0))],
        out_specs=pl.BlockSpec((tm, D // 2), lambda t, te, tv: (t, 0)),
        scratch_shapes=[pltpu.VMEM((F, D), BF16)])
    return pl.pallas_call(
        _moe_down_kernel, name="moe_down",
        grid_spec=grid_spec,
        out_shape=jax.ShapeDtypeStruct((nt * tm, D // 2), jnp.uint32),
        compiler_params=pltpu.CompilerParams(
            dimension_semantics=("arbitrary",),
            vmem_limit_bytes=_vmem_limit(2 * (F * D * 4 + tm * F * 2 + tm * D * 2) + F * D * 2 + tm * D * 4)),
    )(tile_e, tile_valid, hmid, wd)


def _combine_kernel(idx_ref, idxn_ref, x1_ref, gate_ref, g_ref, b_ref, ys_hbm, o_ref, ybuf, sem, *, alpha):
    t = pl.program_id(0)
    nt = pl.num_programs(0)
    tm, D = x1_ref.shape
    half = D // 2
    slot = t % 2

    @pl.when(t == 0)
    def _():
        _start_row_gather(ys_hbm, idx_ref, ybuf.at[0], sem.at[0], TOP_K * tm)

    @pl.when(t + 1 < nt)
    def _():
        _start_row_gather(ys_hbm, idxn_ref, ybuf.at[1 - slot], sem.at[1 - slot], TOP_K * tm)

    _wait_row_gather(ys_hbm, ybuf.at[slot], sem.at[slot], TOP_K * tm)
    gates = gate_ref[...]
    w0 = ybuf[slot, 0:tm, :]
    w1 = ybuf[slot, tm:2 * tm, :]
    unpack_lo = lambda w: pltpu.bitcast(w << 16, F32)
    unpack_hi = lambda w: pltpu.bitcast(w & jnp.uint32(0xFFFF0000), F32)
    g0, g1 = gates[:, 0:1], gates[:, 1:2]
    r_lo = alpha * x1_ref[:, :half] + g0 * unpack_lo(w0) + g1 * unpack_lo(w1)
    r_hi = alpha * x1_ref[:, half:] + g0 * unpack_hi(w0) + g1 * unpack_hi(w1)
    mu = (jnp.sum(r_lo, axis=-1, keepdims=True) + jnp.sum(r_hi, axis=-1, keepdims=True)) / D
    c_lo = r_lo - mu
    c_hi = r_hi - mu
    var = (jnp.sum(c_lo * c_lo, axis=-1, keepdims=True) + jnp.sum(c_hi * c_hi, axis=-1, keepdims=True)) / D
    rstd = lax.rsqrt(var + LN_EPS)
    o_ref[:, :half] = c_lo * rstd * g_ref[:, :half] + b_ref[:, :half]
    o_ref[:, half:] = c_hi * rstd * g_ref[:, half:] + b_ref[:, half:]


def _combine(dest3, x1, row_offset, T, gates, ln_g, ln_b, ys, alpha, tm):
    D = x1.shape[1]
    nt = T // tm
    mo = row_offset // tm
    return pl.pallas_call(
        functools.partial(_combine_kernel, alpha=alpha), name="moe_combine",
        grid=(nt,),
        in_specs=[pl.BlockSpec((1, 1, TOP_K * tm), lambda t: (mo + t, 0, 0), memory_space=pltpu.SMEM),
                  pl.BlockSpec((1, 1, TOP_K * tm), lambda t: (mo + jnp.minimum(t + 1, nt - 1), 0, 0),
                               memory_space=pltpu.SMEM),
                  pl.BlockSpec((tm, D), lambda t: (mo + t, 0)),
                  pl.BlockSpec((tm, TOP_K), lambda t: (mo + t, 0)),
                  pl.BlockSpec((1, D), lambda t: (0, 0)),
                  pl.BlockSpec((1, D), lambda t: (0, 0)),
                  pl.BlockSpec(memory_space=pl.ANY)],
        out_specs=pl.BlockSpec((tm, D), lambda t: (t, 0)),
        out_shape=jax.ShapeDtypeStruct((T, D), F32),
        scratch_shapes=[pltpu.VMEM((2, TOP_K * tm, D // 2), jnp.uint32), pltpu.SemaphoreType.DMA((2,))],
        compiler_params=pltpu.CompilerParams(
            dimension_semantics=("arbitrary",),
            vmem_limit_bytes=_vmem_limit(4 * tm * D * 4 + 2 * TOP_K * tm * D * 2 + 4 * tm * D * 4)),
    )(dest3, dest3, x1, gates, ln_g, ln_b, ys)


def _dispatch_plan(flat_e, n_experts, tile):
    A = flat_e.shape[0]
    onehot = (flat_e[:, None] == jnp.arange(n_experts, dtype=jnp.int32)[None, :])
    blk = 256
    oh3 = onehot.astype(BF16).reshape(A // blk, blk, n_experts)
    tri = (jnp.arange(blk)[:, None] > jnp.arange(blk)[None, :]).astype(BF16)
    local = jnp.einsum('ij,bjk->bik', tri, oh3, preferred_element_type=F32)
    blk_tot = jnp.sum(oh3.astype(F32), axis=1)
    blk_off = jnp.cumsum(blk_tot, axis=0) - blk_tot
    excl = (local + blk_off[:, None, :]).reshape(A, n_experts)
    counts = jnp.sum(blk_tot, axis=0).astype(jnp.int32)
    rank = jnp.sum(jnp.where(onehot, excl, 0.0), axis=1).astype(jnp.int32)
    padded = (counts + tile - 1) // tile * tile
    pend = jnp.cumsum(padded)
    pstart = pend - padded
    dest = (pstart[flat_e] + rank).astype(jnp.int32)
    n_tiles = A // tile + n_experts
    flat_tok = jnp.arange(A, dtype=jnp.int32) // TOP_K
    buf_tok = jnp.zeros((n_tiles * tile,), jnp.int32).at[dest].set(flat_tok, unique_indices=True)
    tile_start = jnp.arange(n_tiles, dtype=jnp.int32) * tile
    tile_e = jnp.minimum(jnp.sum((pend[None, :] <= tile_start[:, None]).astype(jnp.int32), axis=1), n_experts - 1)
    tile_valid = (tile_start < pend[-1]).astype(jnp.int32)
    return dest, buf_tok, tile_e, tile_valid


def _tiles(D, S, C, GW, F):
    max_dil = max(d for _, d in ATT_GROUPS)
    t = {}
    t['ln_rows'] = min(256, S)
    t['proj_rows'] = min(1024, S)
    t['proj_cols'] = min(512, GW)
    t['conv_rows'] = min(256, S)
    t['attn_rows'] = min(2048, S)
    t['out_rows'] = min(256, S)
    t['out_cols'] = min(1024, D)
    t['moe_rows'] = min(256, S)
    t['router_rows'] = min(512, S)
    assert t['proj_rows'] % (16 * max_dil) == 0 and t['attn_rows'] % (HALF_WINDOW * max_dil) == 0
    assert S % t['proj_rows'] == 0 and S % t['attn_rows'] == 0 and S % t['conv_rows'] == 0
    return t


def kernel(x_prompt, x_sample, ln_in_g, ln_in_b, w_in, b_in, conv_w, conv_b, conv_ln_g, conv_ln_b, w_out, b_out,
           ln1_g, ln1_b, w_router_group, b_router_group, w_router_expert, b_router_expert, w_gate, w_up, w_down,
           ln2_g, ln2_b):
    depth = w_in.shape[0]
    assert depth == 1
    assert all(w // (2 * d) == HALF_WINDOW for w, d in ATT_GROUPS)
    alpha = (2.0 * depth) ** 0.25
    D = x_prompt.shape[-1]
    C = conv_w.shape[-1]
    n_att = len(ATT_GROUPS)
    GW = (w_in.shape[-1] - 2 * C) // (3 * n_att)
    H = GW // HEAD_DIM
    n_groups = w_router_group.shape[-1]
    n_experts = w_router_expert.shape[-1]
    F = w_gate.shape[-1]
    row = lambda v: v.reshape(1, -1).astype(F32)

    w_in_b = w_in[0].astype(BF16)
    b_in_r = row(b_in[0])
    w_sections = [(w_in_b[:, :2 * C], b_in_r[:, :2 * C])]
    off = 2 * C
    for _ in range(n_att):
        w_sections.append((w_in_b[:, off:off + 3 * GW], b_in_r[:, off:off + 3 * GW]))
        off += 3 * GW
    w_out_b = w_out[0].astype(BF16)
    wu_out, wo_out = w_out_b[:C], w_out_b[C:]
    RL = LANES
    w_r = jnp.concatenate([w_router_group[0], w_router_expert[0]], axis=1).astype(F32)
    w_r = jnp.pad(w_r, ((0, 0), (0, RL - w_r.shape[1])))
    wrh = w_r.astype(BF16)
    wrl = (w_r - wrh.astype(F32)).astype(BF16)
    b_r = jnp.pad(jnp.concatenate([b_router_group[0], b_router_expert[0]]).astype(F32),
                  (0, RL - n_groups - n_experts)).reshape(1, RL)
    n_heads = n_att * H
    slopes = jnp.exp2(-8.0 * jnp.arange(1, n_heads + 1, dtype=F32) / n_heads).reshape(n_att, H)

    trunks = [x_prompt, x_sample]
    total = sum(x.shape[0] * x.shape[1] for x in trunks)
    staged = []
    offsets = []
    row_offset = 0
    for x in trunks:
        B, S, _ = x.shape
        T = B * S
        t = _tiles(D, S, C, GW, F)
        x2d = x.reshape(T, D)
        xn = _ln_cast(x2d, row(ln_in_g), row(ln_in_b), t['ln_rows'])
        ag = _proj_nat(xn, w_sections[0][0], w_sections[0][1], t['proj_rows'], min(t['proj_cols'], 2 * C))
        u = _conv_module(ag, conv_w[0].astype(F32), row(conv_b[0]), row(conv_ln_g[0]), row(conv_ln_b[0]),
                         B, S, t['conv_rows'])
        qkvs = [_proj_heads(xn, w_sections[1 + g][0], w_sections[1 + g][1], B, S, dil, GW,
                            t['proj_rows'], t['proj_cols'])
                for g, (_, dil) in enumerate(ATT_GROUPS)]
        o = _attention(qkvs, slopes, B, S, H, t['attn_rows'])
        staged.append((x2d, u, o))
        offsets.append((row_offset, T, t))
        row_offset += T
    t0 = offsets[0][2]
    x1 = _outproj(staged, row(ln_in_g), row(ln_in_b), wu_out, wo_out, row(b_out[0]),
                  row(ln1_g[0]), row(ln1_b[0]), alpha, t0['out_rows'], t0['out_cols'])
    gate_l, idx_l = _router(x1, wrh, wrl, b_r, n_groups, n_experts // n_groups, t0['router_rows'])
    gates = gate_l[:, :TOP_K]

    tm = t0['moe_rows']
    dest, buf_tok, tile_e, tile_valid = _dispatch_plan(idx_l[:, :TOP_K].reshape(total * TOP_K), n_experts, tm)
    nt = tile_e.shape[0]
    hmid = _moe_up(tile_e, tile_valid, buf_tok.reshape(nt, 1, tm), x1, w_gate[0].astype(BF16),
                   w_up[0].astype(BF16), tm)
    ys = _moe_down(tile_e, tile_valid, hmid, w_down[0], tm)
    dest3 = dest.reshape(total // tm, tm, TOP_K).transpose(0, 2, 1).reshape(total // tm, 1, TOP_K * tm)
    outs = []
    for x, (ro, T, t) in zip(trunks, offsets):
        y = _combine(dest3, x1, ro, T, gates, row(ln2_g[0]), row(ln2_b[0]), ys, alpha, tm)
        outs.append(y.reshape(x.shape))
    return tuple(outs)
```

```python
import functools

import jax
import jax.numpy as jnp
from jax import lax
from jax.experimental import pallas as pl
from jax.experimental.pallas import tpu as pltpu

F32 = jnp.float32
BF16 = jnp.bfloat16

HEAD_DIM = 128
ATT_GROUPS = ((128, 1), (512, 4), (2048, 16))
HALF_WINDOW = 64
CONV_WIDTH = 31
CONV_HALO = 16
TOP_K = 2
LN_EPS = 1e-5
NEG_INF = -1e30
LANES = 128
V7X_VMEM_BYTES = 64 * 1024 * 1024


def _vmem_limit(nbytes):
    return int(min(max(nbytes * 5 // 4 + (4 << 20), 16 << 20), V7X_VMEM_BYTES - (6 << 20)))


def _ln(x, g, b):
    mu = jnp.mean(x, axis=-1, keepdims=True)
    xc = x - mu
    var = jnp.mean(xc * xc, axis=-1, keepdims=True)
    return xc * lax.rsqrt(var + LN_EPS) * g + b


def _ln_cast_kernel(x_ref, g_ref, b_ref, o_ref):
    o_ref[...] = _ln(x_ref[...], g_ref[...], b_ref[...]).astype(o_ref.dtype)


def _ln_cast(x2d, g, b, tm):
    T, D = x2d.shape
    return pl.pallas_call(
        _ln_cast_kernel, name="ln_cast",
        grid=(T // tm,),
        in_specs=[pl.BlockSpec((tm, D), lambda i: (i, 0)),
                  pl.BlockSpec((1, D), lambda i: (0, 0)),
                  pl.BlockSpec((1, D), lambda i: (0, 0))],
        out_specs=pl.BlockSpec((tm, D), lambda i: (i, 0)),
        out_shape=jax.ShapeDtypeStruct((T, D), BF16),
        compiler_params=pltpu.CompilerParams(
            dimension_semantics=("parallel",),
            vmem_limit_bytes=_vmem_limit(2 * tm * D * 6)),
    )(x2d, g, b)


def _proj_nat_kernel(x_ref, w_ref, b_ref, o_ref):
    acc = jnp.dot(x_ref[...], w_ref[...], preferred_element_type=F32) + b_ref[...]
    o_ref[...] = acc.astype(o_ref.dtype)


def _proj_nat(xn, w, b, tm, tn):
    T, D = xn.shape
    N = w.shape[1]
    return pl.pallas_call(
        _proj_nat_kernel, name="proj_conv",
        grid=(T // tm, N // tn),
        in_specs=[pl.BlockSpec((tm, D), lambda m, n: (m, 0)),
                  pl.BlockSpec((D, tn), lambda m, n: (0, n)),
                  pl.BlockSpec((1, tn), lambda m, n: (0, n))],
        out_specs=pl.BlockSpec((tm, tn), lambda m, n: (m, n)),
        out_shape=jax.ShapeDtypeStruct((T, N), BF16),
        compiler_params=pltpu.CompilerParams(
            dimension_semantics=("parallel", "arbitrary"),
            vmem_limit_bytes=_vmem_limit(2 * (tm * D * 2 + D * tn * 2 + tm * tn * 2) + tm * tn * 4)),
    )(xn, w, b)


def _proj_heads_kernel(x_ref, w_ref, b_ref, *rest, dil, q_blocks, q_scale, has_rider):
    if has_rider:
        ri_ref, o_ref, ro_ref, acc_ref = rest
        ro_ref[...] = ri_ref[...].astype(ro_ref.dtype)
    else:
        o_ref, acc_ref = rest
    n = pl.program_id(2)
    nj, tm, _ = acc_ref.shape
    tn = nj * LANES
    scale = jnp.where(n < q_blocks, q_scale, 1.0).astype(F32)
    if dil == 1:
        acc = (jnp.dot(x_ref[...], w_ref[...], preferred_element_type=F32) + b_ref[...]) * scale
        for j in range(nj):
            o_ref[j, 0, 0] = acc[:, j * LANES:(j + 1) * LANES].astype(o_ref.dtype)
    else:
        hw = tn // 2
        for half in range(2):
            cols = slice(half * hw, (half + 1) * hw)
            acc = (jnp.dot(x_ref[...], w_ref[:, cols], preferred_element_type=F32) + b_ref[:, cols]) * scale
            for j in range(hw // LANES):
                acc_ref[half * (hw // LANES) + j] = acc[:, j * LANES:(j + 1) * LANES]
        for j in range(nj):
            for r in range(dil):
                o_ref[j, 0, r] = acc_ref[j, pl.ds(r, tm // dil, stride=dil), :].astype(o_ref.dtype)


def _proj_heads(xn, w, b, B, S, dil, group_width, tm, tn, rider=None):
    T, D = xn.shape
    N = w.shape[1]
    L = S // dil
    mt = S // tm
    nn = N // tn
    kern = functools.partial(_proj_heads_kernel, dil=dil, q_blocks=group_width // tn,
                             q_scale=HEAD_DIM ** -0.5, has_rider=rider is not None)
    in_specs = [pl.BlockSpec((tm, D), lambda bb, m, n: (bb * mt + m, 0)),
                pl.BlockSpec((D, tn), lambda bb, m, n: (0, n)),
                pl.BlockSpec((1, tn), lambda bb, m, n: (0, n))]
    out_specs = [pl.BlockSpec((tn // LANES, 1, dil, tm // dil, LANES), lambda bb, m, n: (n, bb, 0, m, 0))]
    out_shape = [jax.ShapeDtypeStruct((N // LANES, B, dil, L, LANES), BF16)]
    args = [xn, w, b]
    nbytes = 2 * (tm * D * 2 + D * tn * 2 + tm * tn * 2) + 2 * tm * tn * 4
    if rider is not None:
        w2d, cblk, fc = rider
        rows = w2d.shape[0]
        steps = B * mt * nn
        nblk = 1
        while nblk * 2 <= steps and rows % (nblk * 2) == 0:
            nblk *= 2
        rb = rows // nblk
        rmap = lambda bb, m, n: (jnp.minimum((bb * mt + m) * nn + n, nblk - 1), cblk)
        in_specs.append(pl.BlockSpec((rb, fc), rmap))
        out_specs.append(pl.BlockSpec((rb, fc), lambda bb, m, n: (rmap(bb, m, n)[0], 0)))
        out_shape.append(jax.ShapeDtypeStruct((rows, fc), BF16))
        args.append(w2d)
        nbytes += 2 * rb * fc * 6
    outs = pl.pallas_call(
        kern, name=f"proj_attn_dil{dil}",
        grid=(B, mt, nn),
        in_specs=in_specs,
        out_specs=out_specs,
        out_shape=out_shape,
        scratch_shapes=[pltpu.VMEM((tn // LANES, tm, LANES), F32)],
        compiler_params=pltpu.CompilerParams(
            dimension_semantics=("arbitrary", "arbitrary", "arbitrary"),
            vmem_limit_bytes=_vmem_limit(nbytes)),
    )(*args)
    return outs if rider is not None else outs[0]


def _conv_kernel(cur_ref, prev_ref, next_ref, w_ref, cb_ref, g_ref, b_ref, o_ref, ext_ref, y_ref, sh_ref, *, C):
    i = pl.program_id(1)
    last = pl.num_programs(1) - 1
    tp = cur_ref.shape[1]

    def glu(blk):
        a = blk[:, :C].astype(F32)
        gate = blk[:, C:].astype(F32)
        return a * (1.0 / (1.0 + jnp.exp(-gate)))

    ext_ref[0:CONV_HALO] = glu(prev_ref[0]) * jnp.where(i > 0, 1.0, 0.0)
    ext_ref[CONV_HALO:CONV_HALO + tp] = glu(cur_ref[0])
    ext_ref[CONV_HALO + tp:] = glu(next_ref[0]) * jnp.where(i < last, 1.0, 0.0)

    first = CONV_HALO - CONV_WIDTH // 2

    sub = 8
    span = sh_ref.shape[1]

    def chan_block(cb, carry):
        c0 = pl.multiple_of(cb * LANES, LANES)
        for b in range(1, sub):
            sh_ref[b] = ext_ref[pl.ds(b, span), pl.ds(c0, LANES)]
        acc = jnp.zeros((tp, LANES), F32)
        for d in range(CONV_WIDTH):
            q = first + d
            a8, b = (q // sub) * sub, q % sub
            if b == 0:
                win = ext_ref[pl.ds(a8, tp), pl.ds(c0, LANES)]
            else:
                win = sh_ref[b, pl.ds(a8, tp), :]
            acc = acc + win * w_ref[pl.ds(d, 1), pl.ds(c0, LANES)]
        y_ref[:, pl.ds(c0, LANES)] = acc + cb_ref[:, pl.ds(c0, LANES)]
        return carry

    lax.fori_loop(0, C // LANES, chan_block, 0)
    y = _ln(y_ref[...], g_ref[...], b_ref[...])
    o_ref[0] = (y * (1.0 / (1.0 + jnp.exp(-y)))).astype(o_ref.dtype)


def _conv_module(ag, conv_w, conv_b, ln_g, ln_b, B, S, tp):
    C = ag.shape[1] // 2
    ag3 = ag.reshape(B, S, 2 * C)
    hb = tp // CONV_HALO
    nh = S // CONV_HALO
    out = pl.pallas_call(
        functools.partial(_conv_kernel, C=C), name="conv_module",
        grid=(B, S // tp),
        in_specs=[pl.BlockSpec((1, tp, 2 * C), lambda b, i: (b, i, 0)),
                  pl.BlockSpec((1, CONV_HALO, 2 * C), lambda b, i: (b, jnp.maximum(i * hb - 1, 0), 0)),
                  pl.BlockSpec((1, CONV_HALO, 2 * C), lambda b, i: (b, jnp.minimum((i + 1) * hb, nh - 1), 0)),
                  pl.BlockSpec((CONV_WIDTH, C), lambda b, i: (0, 0)),
                  pl.BlockSpec((1, C), lambda b, i: (0, 0)),
                  pl.BlockSpec((1, C), lambda b, i: (0, 0)),
                  pl.BlockSpec((1, C), lambda b, i: (0, 0))],
        out_specs=pl.BlockSpec((1, tp, C), lambda b, i: (b, i, 0)),
        out_shape=jax.ShapeDtypeStruct((B, S, C), BF16),
        scratch_shapes=[pltpu.VMEM((tp + 2 * CONV_HALO, C), F32), pltpu.VMEM((tp, C), F32),
                        pltpu.VMEM((8, tp + 2 * CONV_HALO - 8, LANES), F32)],
        compiler_params=pltpu.CompilerParams(
            dimension_semantics=("parallel", "parallel"),
            vmem_limit_bytes=_vmem_limit(2 * tp * 2 * C * 2 + 2 * tp * C * 2 + 3 * tp * C * 4)),
    )(ag3, ag3, ag3, conv_w, conv_b, ln_g, ln_b)
    return out.reshape(B * S, C)


def _attn_kernel(slopes_ref, *refs, tp, dils, heads):
    ng = len(dils)
    ins = refs[:7 * ng]
    o_ref = refs[7 * ng]
    scr = refs[7 * ng + 1:]
    kwins, vwins = scr[:ng], scr[ng:2 * ng]
    o_nat, m_nat, l_nat = scr[2 * ng:2 * ng + 3]

    h = pl.program_id(1)
    i = pl.program_id(2)
    W = HALF_WINDOW

    for g, dil in enumerate(dils):
        q_ref, kc, kp, kn, vc, vp, vn = ins[7 * g:7 * g + 7]
        kwin, vwin = kwins[g], vwins[g]
        n = tp // dil
        L = n * pl.num_programs(2)
        sq = min(128, n)
        nsub = n // sq
        for win, cur, prev, nxt in ((kwin, kc, kp, kn), (vwin, vc, vp, vn)):
            win[:, 0:W] = prev[0, 0]
            win[:, W:W + n] = cur[0, 0]
            win[:, W + n:] = nxt[0, 0]

        slope = slopes_ref[g, h] * float(dil)
        row = lax.broadcasted_iota(jnp.int32, (sq, sq + 2 * W), 0)
        col = lax.broadcasted_iota(jnp.int32, (sq, sq + 2 * W), 1)
        dist = jnp.abs(col - W - row)
        bias = jnp.where(dist <= W, -slope * dist.astype(F32), NEG_INF)
        col1 = lax.broadcasted_iota(jnp.int32, (1, sq + 2 * W), 1)

        for r in range(dil):
            for sub in range(nsub):
                qs = q_ref[0, 0, r, sub * sq:(sub + 1) * sq, :]
                kw = kwin[r, sub * sq:sub * sq + sq + 2 * W, :]
                vw = vwin[r, sub * sq:sub * sq + sq + 2 * W, :]
                s = lax.dot_general(qs, kw, (((1,), (1,)), ((), ())), preferred_element_type=F32) + bias
                if sub == 0 or sub == nsub - 1:
                    kidx = i * n + (sub * sq - W) + col1
                    s = s + jnp.where((kidx >= 0) & (kidx < L), 0.0, NEG_INF)
                m = jnp.max(s, axis=-1, keepdims=True)
                p = jnp.exp(s - m)
                l = jnp.sum(p, axis=-1, keepdims=True)
                o = jnp.dot(p.astype(BF16), vw, preferred_element_type=F32)
                if dil == 1:
                    rows = pl.ds(sub * sq, sq)
                else:
                    rows = pl.ds(sub * sq * dil + r, sq, stride=dil)
                o_nat[g, rows, :] = o
                m_nat[g, rows, :] = jnp.broadcast_to(m, (sq, LANES))
                l_nat[g, rows, :] = jnp.broadcast_to(l, (sq, LANES))

    ch = min(128, tp)

    def merge(c, carry):
        r0 = pl.multiple_of(c * ch, ch)
        ms = [m_nat[g, pl.ds(r0, ch), :] for g in range(ng)]
        mx = functools.reduce(jnp.maximum, ms)
        num = jnp.zeros((ch, LANES), F32)
        den = jnp.zeros((ch, LANES), F32)
        for g in range(ng):
            wg = jnp.exp(ms[g] - mx)
            num = num + wg * o_nat[g, pl.ds(r0, ch), :]
            den = den + wg * l_nat[g, pl.ds(r0, ch), :]
        o_ref[0, pl.ds(r0, ch), :] = (num / den).astype(o_ref.dtype)
        return carry

    lax.fori_loop(0, tp // ch, merge, 0)


def _attention(qkvs, slopes, B, S, heads, tp):
    dils = tuple(d for _, d in ATT_GROUPS)
    W = HALF_WINDOW
    H = heads
    in_specs = [pl.BlockSpec(memory_space=pltpu.SMEM)]
    args = [slopes]
    scratch_k, scratch_v = [], []
    nbytes = 0
    for g, dil in enumerate(dils):
        n = tp // dil
        L = S // dil
        hb = n // W
        nh = L // W

        def cur_map(which):
            return lambda b, h, i: (which * H + h, b, 0, i, 0)

        def prev_map(which, hb=hb):
            return lambda b, h, i: (which * H + h, b, 0, jnp.maximum(i * hb - 1, 0), 0)

        def next_map(which, hb=hb, nh=nh):
            return lambda b, h, i: (which * H + h, b, 0, jnp.minimum((i + 1) * hb, nh - 1), 0)

        blk = (1, 1, dil, n, LANES)
        halo = (1, 1, dil, W, LANES)
        in_specs += [pl.BlockSpec(blk, cur_map(0)),
                     pl.BlockSpec(blk, cur_map(1)), pl.BlockSpec(halo, prev_map(1)), pl.BlockSpec(halo, next_map(1)),
                     pl.BlockSpec(blk, cur_map(2)), pl.BlockSpec(halo, prev_map(2)), pl.BlockSpec(halo, next_map(2))]
        args += [qkvs[g]] * 7
        scratch_k.append(pltpu.VMEM((dil, n + 2 * W, LANES), BF16))
        scratch_v.append(pltpu.VMEM((dil, n + 2 * W, LANES), BF16))
        nbytes += 2 * (3 * tp + 4 * dil * W) * LANES * 2 + 2 * (tp + 2 * W * dil) * LANES * 2
    nbytes += 9 * tp * LANES * 4 + 2 * tp * LANES * 2
    ng = len(dils)
    out = pl.pallas_call(
        functools.partial(_attn_kernel, tp=tp, dils=dils, heads=H), name="attention",
        grid=(B, H, S // tp),
        in_specs=in_specs,
        out_specs=pl.BlockSpec((1, tp, LANES), lambda b, h, i: (b, i, h)),
        out_shape=jax.ShapeDtypeStruct((B, S, H * LANES), BF16),
        scratch_shapes=scratch_k + scratch_v + [pltpu.VMEM((ng, tp, LANES), F32)] * 3,
        compiler_params=pltpu.CompilerParams(
            dimension_semantics=("parallel", "parallel", "parallel"),
            vmem_limit_bytes=_vmem_limit(nbytes)),
    )(*args)
    return out.reshape(B * S, H * LANES)


def _ln_rows(v, g_ref, b_ref):
    mu = jnp.mean(v, axis=-1, keepdims=True)
    c = v - mu
    var = jnp.mean(c * c, axis=-1, keepdims=True)
    return c * lax.rsqrt(var + LN_EPS) * g_ref[...] + b_ref[...]


def _outproj_kernel(*refs, alpha, tile_starts, tn):
    nk = len(tile_starts) - 1
    trunk_refs = [refs[3 * k:3 * k + 3] for k in range(nk)]
    (lig_ref, lib_ref, wu_ref, wo_ref, bo_ref, g1_ref, b1_ref,
     x1_ref, acc_ref, us_ref, os_ref) = refs[3 * nk:]
    m = pl.program_id(0)
    n = pl.program_id(1)
    tm = acc_ref.shape[1]
    chunk = tm // (acc_ref.shape[2] // tn)
    sub = 8
    s = m % 2

    @pl.when((m == 0) & (n == 0))
    def _():
        acc_ref[...] = jnp.zeros_like(acc_ref)
        us_ref[...] = jnp.zeros_like(us_ref)
        os_ref[...] = jnp.zeros_like(os_ref)

    def body(x_ref, u_ref, o_ref):
        r0 = pl.multiple_of(n * chunk, chunk)
        for g in range(chunk // sub):
            rows = pl.ds(r0 + g * sub, sub)
            x1_ref[rows, :] = _ln_rows(acc_ref[s, rows, :], g1_ref, b1_ref)
            acc_ref[s, rows, :] = alpha * _ln_rows(x_ref[rows, :], lig_ref, lib_ref)
        us_ref[s, pl.ds(r0, chunk), :] = u_ref[pl.ds(r0, chunk), :]
        os_ref[s, pl.ds(r0, chunk), :] = o_ref[pl.ds(r0, chunk), :]
        c0 = pl.multiple_of(n * tn, LANES)
        mix = (jnp.dot(us_ref[1 - s], wu_ref[:, pl.ds(c0, tn)], preferred_element_type=F32)
               + jnp.dot(os_ref[1 - s], wo_ref[:, pl.ds(c0, tn)], preferred_element_type=F32)
               + bo_ref[:, pl.ds(c0, tn)])
        acc_ref[1 - s, :, pl.ds(c0, tn)] = acc_ref[1 - s, :, pl.ds(c0, tn)] + mix

    for k, (x_ref, u_ref, o_ref) in enumerate(trunk_refs):
        lo = tile_starts[k]
        cond = (m >= lo) if k == nk - 1 else ((m >= lo) & (m < tile_starts[k + 1]))
        pl.when(cond)(functools.partial(body, x_ref, u_ref, o_ref))


def _outproj(trunk_inputs, ln_in_g, ln_in_b, wu, wo, b_out, ln1_g, ln1_b, alpha, tm, tn):
    D = trunk_inputs[0][0].shape[1]
    C = wu.shape[0]
    GW = wo.shape[0]
    tile_starts = [0]
    for x, _, _ in trunk_inputs:
        tile_starts.append(tile_starts[-1] + x.shape[0] // tm)
    nm = tile_starts[-1]
    const = lambda m, n: (0, 0)
    in_specs, args = [], []
    for k, (x, u, o) in enumerate(trunk_inputs):
        def rows(m, n, lo=tile_starts[k], cnt=tile_starts[k + 1] - tile_starts[k]):
            return (jnp.clip(m - lo, 0, cnt - 1), 0)
        in_specs += [pl.BlockSpec((tm, D), rows), pl.BlockSpec((tm, C), rows), pl.BlockSpec((tm, GW), rows)]
        args += [x, u, o]
    in_specs += [pl.BlockSpec((1, D), const), pl.BlockSpec((1, D), const),
                 pl.BlockSpec((C, D), const, pipeline_mode=pl.Buffered(1)),
                 pl.BlockSpec((GW, D), const, pipeline_mode=pl.Buffered(1)),
                 pl.BlockSpec((1, D), const),
                 pl.BlockSpec((1, D), const), pl.BlockSpec((1, D), const)]
    args += [ln_in_g, ln_in_b, wu, wo, b_out, ln1_g, ln1_b]
    nk = len(trunk_inputs)
    nbytes = (2 * (nk * (tm * D * 4 + tm * (C + GW) * 2) + tm * D * 4) + (C + GW) * D * 2
              + 2 * tm * D * 4 + 2 * tm * (C + GW) * 2)
    return pl.pallas_call(
        functools.partial(_outproj_kernel, alpha=alpha, tile_starts=tuple(tile_starts), tn=tn), name="outproj",
        grid=(nm + 2, D // tn),
        in_specs=in_specs,
        out_specs=pl.BlockSpec((tm, D), lambda m, n: (jnp.clip(m - 2, 0, nm - 1), 0)),
        out_shape=jax.ShapeDtypeStruct((nm * tm, D), F32),
        scratch_shapes=[pltpu.VMEM((2, tm, D), F32), pltpu.VMEM((2, tm, C), BF16), pltpu.VMEM((2, tm, GW), BF16)],
        compiler_params=pltpu.CompilerParams(
            dimension_semantics=("arbitrary", "arbitrary"),
            vmem_limit_bytes=_vmem_limit(nbytes)),
    )(*args)


def _router_kernel(x_ref, wh_ref, wl_ref, b_ref, gate_ref, idx_ref, *, n_groups, per):
    x = x_ref[...]
    xh = x.astype(BF16)
    xl = (x - xh.astype(F32)).astype(BF16)
    wh = wh_ref[...]
    lg = (jnp.dot(xh, wh, preferred_element_type=F32) + jnp.dot(xl, wh, preferred_element_type=F32)
          + jnp.dot(xh, wl_ref[...], preferred_element_type=F32) + b_ref[...])
    lane = lax.broadcasted_iota(jnp.int32, lg.shape, 1)
    big = jnp.int32(lg.shape[1])

    def top1(v):
        mx = jnp.max(v, axis=-1, keepdims=True)
        return mx, jnp.min(jnp.where(v == mx, lane, big), axis=-1, keepdims=True)

    gl = jnp.where(lane < n_groups, lg, NEG_INF)
    gmax, g_star = top1(gl)
    p_group = 1.0 / jnp.sum(jnp.exp(gl - gmax), axis=-1, keepdims=True)
    first = n_groups + g_star * per
    el = jnp.where((lane >= first) & (lane < first + per), lg, NEG_INF)
    v0, i0 = top1(el)
    v1, i1 = top1(jnp.where(lane == i0, NEG_INF, el))
    e1 = jnp.exp(v1 - v0)
    g0 = p_group / (1.0 + e1)
    g1 = g0 * e1
    gate_ref[...] = jnp.where(lane == 0, g0, jnp.where(lane == 1, g1, 0.0))
    idx_ref[...] = jnp.where(lane == 0, i0 - n_groups, jnp.where(lane == 1, i1 - n_groups, 0))


def _router(x1, wrh, wrl, br, n_groups, per, tm):
    T, D = x1.shape
    RL = wrh.shape[1]
    const = lambda i: (0, 0)
    return pl.pallas_call(
        functools.partial(_router_kernel, n_groups=n_groups, per=per), name="router",
        grid=(T // tm,),
        in_specs=[pl.BlockSpec((tm, D), lambda i: (i, 0)),
                  pl.BlockSpec((D, RL), const), pl.BlockSpec((D, RL), const), pl.BlockSpec((1, RL), const)],
        out_specs=[pl.BlockSpec((tm, RL), lambda i: (i, 0)), pl.BlockSpec((tm, RL), lambda i: (i, 0))],
        out_shape=[jax.ShapeDtypeStruct((T, RL), F32), jax.ShapeDtypeStruct((T, RL), jnp.int32)],
        compiler_params=pltpu.CompilerParams(
            dimension_semantics=("parallel",),
            vmem_limit_bytes=_vmem_limit(2 * tm * D * 4 + tm * D * 4 + 4 * D * RL * 2 + 4 * tm * RL * 4)),
    )(x1, wrh, wrl, br)


def _start_row_gather(src_hbm, idx_ref, dst, sem, n_rows):
    for r in range(n_rows):
        pltpu.make_async_copy(src_hbm.at[pl.ds(idx_ref[0, 0, r], 1)], dst.at[pl.ds(r, 1)], sem).start()


def _wait_row_gather(src_hbm, dst, sem, n_rows):
    for r in range(n_rows):
        pltpu.make_async_copy(src_hbm.at[pl.ds(0, 1)], dst.at[pl.ds(r, 1)], sem).wait()


def _expert_changed(te_ref, t):
    return (t == 0) | (te_ref[t] != te_ref[jnp.maximum(t - 1, 0)])


def _moe_up_kernel(te_ref, tv_ref, idx_ref, idxn_ref, x_hbm, *rest, n_pieces):
    wg_refs, wu_refs = rest[:n_pieces], rest[n_pieces:2 * n_pieces]
    h_ref, xbuf, sem = rest[2 * n_pieces:]
    t = pl.program_id(0)
    nt = pl.num_programs(0)
    tm = xbuf.shape[1]
    slot = t % 2

    @pl.when((t == 0) & (tv_ref[0] > 0))
    def _():
        _start_row_gather(x_hbm, idx_ref, xbuf.at[0], sem.at[0], tm)

    nxt = jnp.minimum(t + 1, nt - 1)

    @pl.when((t + 1 < nt) & (tv_ref[nxt] > 0))
    def _():
        _start_row_gather(x_hbm, idxn_ref, xbuf.at[1 - slot], sem.at[1 - slot], tm)

    @pl.when(tv_ref[t] > 0)
    def _():
        _wait_row_gather(x_hbm, xbuf.at[slot], sem.at[slot], tm)
        xb = xbuf[slot].astype(BF16)
        fc = h_ref.shape[1] // n_pieces
        for p in range(n_pieces):
            a = jnp.dot(xb, wg_refs[p][0], preferred_element_type=F32)
            b = jnp.dot(xb, wu_refs[p][0], preferred_element_type=F32)
            h_ref[:, p * fc:(p + 1) * fc] = (a * (1.0 / (1.0 + jnp.exp(-a))) * b).astype(h_ref.dtype)

    @pl.when(tv_ref[t] == 0)
    def _():
        h_ref[...] = jnp.zeros_like(h_ref)


def _moe_up(tile_e, tile_valid, buf_tok3, x1, wg_pieces, wu_pieces, tm):
    nt = tile_e.shape[0]
    n_pieces = len(wg_pieces)
    E, D, fc = wg_pieces[0].shape
    F = fc * n_pieces
    wspec = pl.BlockSpec((1, D, fc), lambda t, te, tv: (te[t], 0, 0))
    grid_spec = pltpu.PrefetchScalarGridSpec(
        num_scalar_prefetch=2,
        grid=(nt,),
        in_specs=[pl.BlockSpec((1, 1, tm), lambda t, te, tv: (t, 0, 0), memory_space=pltpu.SMEM),
                  pl.BlockSpec((1, 1, tm), lambda t, te, tv: (jnp.minimum(t + 1, nt - 1), 0, 0),
                               memory_space=pltpu.SMEM),
                  pl.BlockSpec(memory_space=pl.ANY)] + [wspec] * (2 * n_pieces),
        out_specs=pl.BlockSpec((tm, F), lambda t, te, tv: (t, 0)),
        scratch_shapes=[pltpu.VMEM((2, tm, D), F32), pltpu.SemaphoreType.DMA((2,))])
    return pl.pallas_call(
        functools.partial(_moe_up_kernel, n_pieces=n_pieces), name="moe_up",
        grid_spec=grid_spec,
        out_shape=jax.ShapeDtypeStruct((nt * tm, F), BF16),
        compiler_params=pltpu.CompilerParams(
            dimension_semantics=("arbitrary",),
            vmem_limit_bytes=_vmem_limit(4 * D * F * 2 + 2 * tm * D * 4 + 2 * tm * F * 2 + tm * D * 2)),
    )(tile_e, tile_valid, buf_tok3, buf_tok3, x1, *wg_pieces, *wu_pieces)


def _bf16_bits(v):
    u = pltpu.bitcast(v, jnp.uint32)
    return u + jnp.uint32(0x7FFF) + ((u >> 16) & jnp.uint32(1))


def _moe_down_kernel(te_ref, tv_ref, h_ref, wd_ref, y_ref, wdb_ref):
    t = pl.program_id(0)
    valid = tv_ref[t] > 0
    half = y_ref.shape[1]

    @pl.when(valid & _expert_changed(te_ref, t))
    def _():
        wdb_ref[...] = wd_ref[0].astype(BF16)

    @pl.when(valid)
    def _():
        y = jnp.dot(h_ref[...], wdb_ref[...], preferred_element_type=F32)
        lo = _bf16_bits(y[:, :half]) >> 16
        hi = _bf16_bits(y[:, half:]) & jnp.uint32(0xFFFF0000)
        y_ref[...] = lo | hi

    @pl.when(jnp.logical_not(valid))
    def _():
        y_ref[...] = jnp.zeros_like(y_ref)


def _moe_down(tile_e, tile_valid, hmid, wd, tm):
    nt = tile_e.shape[0]
    E, F, D = wd.shape
    grid_spec = pltpu.PrefetchScalarGridSpec(
        num_scalar_prefetch=2,
        grid=(nt,),
        in_specs=[pl.BlockSpec((tm, F), lambda t, te, tv: (t, 0)),
                  pl.BlockSpec((1, F, D), lambda t, te, tv: (te[t], 0, 0))],
        out_specs=pl.BlockSpec((tm, D // 2), lambda t, te, tv: (t, 0)),
        scratch_shapes=[pltpu.VMEM((F, D), BF16)])
    return pl.pallas_call(
        _moe_down_kernel, name="moe_down",
        grid_spec=grid_spec,
        out_shape=jax.ShapeDtypeStruct((nt * tm, D // 2), jnp.uint32),
        compiler_params=pltpu.CompilerParams(
            dimension_semantics=("arbitrary",),
            vmem_limit_bytes=_vmem_limit(2 * (F * D * 4 + tm * F * 2 + tm * D * 2) + F * D * 2 + tm * D * 4)),
    )(tile_e, tile_valid, hmid, wd)


def _combine_kernel(idx_ref, idxn_ref, x1_ref, gate_ref, g_ref, b_ref, ys_hbm, o_ref, ybuf, sem, *, alpha):
    t = pl.program_id(0)
    nt = pl.num_programs(0)
    tm, D = x1_ref.shape
    half = D // 2
    slot = t % 2

    @pl.when(t == 0)
    def _():
        _start_row_gather(ys_hbm, idx_ref, ybuf.at[0], sem.at[0], TOP_K * tm)

    @pl.when(t + 1 < nt)
    def _():
        _start_row_gather(ys_hbm, idxn_ref, ybuf.at[1 - slot], sem.at[1 - slot], TOP_K * tm)

    _wait_row_gather(ys_hbm, ybuf.at[slot], sem.at[slot], TOP_K * tm)
    gates = gate_ref[...]
    w0 = ybuf[slot, 0:tm, :]
    w1 = ybuf[slot, tm:2 * tm, :]
    unpack_lo = lambda w: pltpu.bitcast(w << 16, F32)
    unpack_hi = lambda w: pltpu.bitcast(w & jnp.uint32(0xFFFF0000), F32)
    g0, g1 = gates[:, 0:1], gates[:, 1:2]
    r_lo = alpha * x1_ref[:, :half] + g0 * unpack_lo(w0) + g1 * unpack_lo(w1)
    r_hi = alpha * x1_ref[:, half:] + g0 * unpack_hi(w0) + g1 * unpack_hi(w1)
    mu = (jnp.sum(r_lo, axis=-1, keepdims=True) + jnp.sum(r_hi, axis=-1, keepdims=True)) / D
    c_lo = r_lo - mu
    c_hi = r_hi - mu
    var = (jnp.sum(c_lo * c_lo, axis=-1, keepdims=True) + jnp.sum(c_hi * c_hi, axis=-1, keepdims=True)) / D
    rstd = lax.rsqrt(var + LN_EPS)
    o_ref[:, :half] = c_lo * rstd * g_ref[:, :half] + b_ref[:, :half]
    o_ref[:, half:] = c_hi * rstd * g_ref[:, half:] + b_ref[:, half:]


def _combine(dest3, x1, row_offset, T, gates, ln_g, ln_b, ys, alpha, tm):
    D = x1.shape[1]
    nt = T // tm
    mo = row_offset // tm
    return pl.pallas_call(
        functools.partial(_combine_kernel, alpha=alpha), name="moe_combine",
        grid=(nt,),
        in_specs=[pl.BlockSpec((1, 1, TOP_K * tm), lambda t: (mo + t, 0, 0), memory_space=pltpu.SMEM),
                  pl.BlockSpec((1, 1, TOP_K * tm), lambda t: (mo + jnp.minimum(t + 1, nt - 1), 0, 0),
                               memory_space=pltpu.SMEM),
                  pl.BlockSpec((tm, D), lambda t: (mo + t, 0)),
                  pl.BlockSpec((tm, TOP_K), lambda t: (mo + t, 0)),
                  pl.BlockSpec((1, D), lambda t: (0, 0)),
                  pl.BlockSpec((1, D), lambda t: (0, 0)),
                  pl.BlockSpec(memory_space=pl.ANY)],
        out_specs=pl.BlockSpec((tm, D), lambda t: (t, 0)),
        out_shape=jax.ShapeDtypeStruct((T, D), F32),
        scratch_shapes=[pltpu.VMEM((2, TOP_K * tm, D // 2), jnp.uint32), pltpu.SemaphoreType.DMA((2,))],
        compiler_params=pltpu.CompilerParams(
            dimension_semantics=("arbitrary",),
            vmem_limit_bytes=_vmem_limit(4 * tm * D * 4 + 2 * TOP_K * tm * D * 2 + 4 * tm * D * 4)),
    )(dest3, dest3, x1, gates, ln_g, ln_b, ys)


def _dispatch_plan(flat_e, n_experts, tile):
    A = flat_e.shape[0]
    onehot = (flat_e[:, None] == jnp.arange(n_experts, dtype=jnp.int32)[None, :])
    blk = 256
    oh3 = onehot.astype(BF16).reshape(A // blk, blk, n_experts)
    tri = (jnp.arange(blk)[:, None] > jnp.arange(blk)[None, :]).astype(BF16)
    local = jnp.einsum('ij,bjk->bik', tri, oh3, preferred_element_type=F32)
    blk_tot = jnp.sum(oh3.astype(F32), axis=1)
    blk_off = jnp.cumsum(blk_tot, axis=0) - blk_tot
    excl = (local + blk_off[:, None, :]).reshape(A, n_experts)
    counts = jnp.sum(blk_tot, axis=0).astype(jnp.int32)
    rank = jnp.sum(jnp.where(onehot, excl, 0.0), axis=1).astype(jnp.int32)
    padded = (counts + tile - 1) // tile * tile
    pend = jnp.cumsum(padded)
    pstart = pend - padded
    dest = (pstart[flat_e] + rank).astype(jnp.int32)
    n_tiles = A // tile + n_experts
    flat_tok = jnp.arange(A, dtype=jnp.int32) // TOP_K
    buf_tok = jnp.zeros((n_tiles * tile,), jnp.int32).at[dest].set(flat_tok, unique_indices=True)
    tile_start = jnp.arange(n_tiles, dtype=jnp.int32) * tile
    tile_e = jnp.minimum(jnp.sum((pend[None, :] <= tile_start[:, None]).astype(jnp.int32), axis=1), n_experts - 1)
    tile_valid = (tile_start < pend[-1]).astype(jnp.int32)
    return dest, buf_tok, tile_e, tile_valid


def _tiles(D, S, C, GW, F):
    max_dil = max(d for _, d in ATT_GROUPS)
    t = {}
    t['ln_rows'] = min(256, S)
    t['proj_rows'] = min(1024, S)
    t['proj_cols'] = min(512, GW)
    t['conv_rows'] = min(256, S)
    t['attn_rows'] = min(2048, S)
    t['out_rows'] = min(256, S)
    t['out_cols'] = min(1024, D)
    t['moe_rows'] = min(256, S)
    t['router_rows'] = min(512, S)
    assert t['proj_rows'] % (16 * max_dil) == 0 and t['attn_rows'] % (HALF_WINDOW * max_dil) == 0
    assert S % t['proj_rows'] == 0 and S % t['attn_rows'] == 0 and S % t['conv_rows'] == 0
    return t


def kernel(x_prompt, x_sample, ln_in_g, ln_in_b, w_in, b_in, conv_w, conv_b, conv_ln_g, conv_ln_b, w_out, b_out,
           ln1_g, ln1_b, w_router_group, b_router_group, w_router_expert, b_router_expert, w_gate, w_up, w_down,
           ln2_g, ln2_b):
    depth = w_in.shape[0]
    assert depth == 1
    assert all(w // (2 * d) == HALF_WINDOW for w, d in ATT_GROUPS)
    alpha = (2.0 * depth) ** 0.25
    D = x_prompt.shape[-1]
    C = conv_w.shape[-1]
    n_att = len(ATT_GROUPS)
    GW = (w_in.shape[-1] - 2 * C) // (3 * n_att)
    H = GW // HEAD_DIM
    n_groups = w_router_group.shape[-1]
    n_experts = w_router_expert.shape[-1]
    F = w_gate.shape[-1]
    row = lambda v: v.reshape(1, -1).astype(F32)

    w_in_b = w_in[0].astype(BF16)
    b_in_r = row(b_in[0])
    w_sections = [(w_in_b[:, :2 * C], b_in_r[:, :2 * C])]
    off = 2 * C
    for _ in range(n_att):
        w_sections.append((w_in_b[:, off:off + 3 * GW], b_in_r[:, off:off + 3 * GW]))
        off += 3 * GW
    w_out_b = w_out[0].astype(BF16)
    wu_out, wo_out = w_out_b[:C], w_out_b[C:]
    RL = LANES
    w_r = jnp.concatenate([w_router_group[0], w_router_expert[0]], axis=1).astype(F32)
    w_r = jnp.pad(w_r, ((0, 0), (0, RL - w_r.shape[1])))
    wrh = w_r.astype(BF16)
    wrl = (w_r - wrh.astype(F32)).astype(BF16)
    b_r = jnp.pad(jnp.concatenate([b_router_group[0], b_router_expert[0]]).astype(F32),
                  (0, RL - n_groups - n_experts)).reshape(1, RL)
    n_heads = n_att * H
    slopes = jnp.exp2(-8.0 * jnp.arange(1, n_heads + 1, dtype=F32) / n_heads).reshape(n_att, H)

    n_fp = 2 if F % (2 * LANES) == 0 else 1
    fc = F // n_fp
    riders = [(wmat[0].reshape(n_experts * D, F), c, fc, (name, c))
              for c in range(n_fp) for name, wmat in (('gate', w_gate), ('up', w_up))]
    n_rider_hosts = (len(riders) + 1) // 2
    expert_bf16 = {}
    trunks = [x_prompt, x_sample]
    total = sum(x.shape[0] * x.shape[1] for x in trunks)
    staged = []
    offsets = []
    row_offset = 0
    for x in trunks:
        B, S, _ = x.shape
        T = B * S
        t = _tiles(D, S, C, GW, F)
        x2d = x.reshape(T, D)
        xn = _ln_cast(x2d, row(ln_in_g), row(ln_in_b), t['ln_rows'])
        ag = _proj_nat(xn, w_sections[0][0], w_sections[0][1], t['proj_rows'], min(t['proj_cols'], 2 * C))
        u = _conv_module(ag, conv_w[0].astype(F32), row(conv_b[0]), row(conv_ln_g[0]), row(conv_ln_b[0]),
                         B, S, t['conv_rows'])
        qkvs = []
        for g, (_, dil) in enumerate(ATT_GROUPS):
            rider = riders.pop(0) if (riders and g < n_rider_hosts) else None
            res = _proj_heads(xn, w_sections[1 + g][0], w_sections[1 + g][1], B, S, dil, GW,
                              t['proj_rows'], t['proj_cols'], rider=None if rider is None else rider[:3])
            if rider is not None:
                expert_bf16[rider[3]] = res[1].reshape(n_experts, D, rider[2])
                res = res[0]
            qkvs.append(res)
        o = _attention(qkvs, slopes, B, S, H, t['attn_rows'])
        staged.append((x2d, u, o))
        offsets.append((row_offset, T, t))
        row_offset += T
    t0 = offsets[0][2]
    x1 = _outproj(staged, row(ln_in_g), row(ln_in_b), wu_out, wo_out, row(b_out[0]),
                  row(ln1_g[0]), row(ln1_b[0]), alpha, t0['out_rows'], t0['out_cols'])
    gate_l, idx_l = _router(x1, wrh, wrl, b_r, n_groups, n_experts // n_groups, t0['router_rows'])
    gates = gate_l[:, :TOP_K]

    tm = t0['moe_rows']
    dest, buf_tok, tile_e, tile_valid = _dispatch_plan(idx_l[:, :TOP_K].reshape(total * TOP_K), n_experts, tm)
    nt = tile_e.shape[0]
    hmid = _moe_up(tile_e, tile_valid, buf_tok.reshape(nt, 1, tm), x1,
                   [expert_bf16[('gate', c)] for c in range(n_fp)],
                   [expert_bf16[('up', c)] for c in range(n_fp)], tm)
    ys = _moe_down(tile_e, tile_valid, hmid, w_down[0], tm)
    dest3 = dest.reshape(total // tm, tm, TOP_K).transpose(0, 2, 1).reshape(total // tm, 1, TOP_K * tm)
    outs = []
    for x, (ro, T, t) in zip(trunks, offsets):
        y = _combine(dest3, x1, ro, T, gates, row(ln2_g[0]), row(ln2_b[0]), ys, alpha, tm)
        outs.append(y.reshape(x.shape))
    return tuple(outs)
```

```python
import functools

import jax
import jax.numpy as jnp
from jax import lax
from jax.experimental import pallas as pl
from jax.experimental.pallas import tpu as pltpu

F32 = jnp.float32
BF16 = jnp.bfloat16

HEAD_DIM = 128
ATT_GROUPS = ((128, 1), (512, 4), (2048, 16))
HALF_WINDOW = 64
CONV_WIDTH = 31
CONV_HALO = 16
TOP_K = 2
LN_EPS = 1e-5
NEG_INF = -1e30
LANES = 128
V7X_VMEM_BYTES = 64 * 1024 * 1024


def _vmem_limit(nbytes):
    return int(min(max(nbytes * 5 // 4 + (4 << 20), 16 << 20), V7X_VMEM_BYTES - (6 << 20)))


def _ln(x, g, b):
    mu = jnp.mean(x, axis=-1, keepdims=True)
    xc = x - mu
    var = jnp.mean(xc * xc, axis=-1, keepdims=True)
    return xc * lax.rsqrt(var + LN_EPS) * g + b


def _ln_rows(v, g_ref, b_ref):
    mu = jnp.mean(v, axis=-1, keepdims=True)
    c = v - mu
    var = jnp.mean(c * c, axis=-1, keepdims=True)
    return c * lax.rsqrt(var + LN_EPS) * g_ref[...] + b_ref[...]


def _by_parity(m, fn, buf_a, buf_b):
    pl.when(m % 2 == 0)(functools.partial(fn, buf_a, buf_b))
    pl.when(m % 2 == 1)(functools.partial(fn, buf_b, buf_a))


def _ln_proj_kernel(x_ref, g_ref, bt_ref, w_ref, b_ref, xn_ref, o_ref, xa_ref, xb_ref, *, nn):
    m = pl.program_id(0)
    n = pl.program_id(1)
    tm = xa_ref.shape[0]
    chunk = tm // nn
    sub = 16

    @pl.when((m == 0) & (n == 0))
    def _():
        xb_ref[...] = jnp.zeros_like(xb_ref)

    def body(stage_ref, work_ref):
        acc = jnp.dot(work_ref[...], w_ref[...], preferred_element_type=F32) + b_ref[...]
        o_ref[...] = acc.astype(o_ref.dtype)
        r0 = pl.multiple_of(n * chunk, chunk)
        for g in range(chunk // sub):
            rows = pl.ds(r0 + g * sub, sub)
            lo = _ln_rows(x_ref[pl.ds(r0 + g * sub, 8), :], g_ref, bt_ref)
            hi = _ln_rows(x_ref[pl.ds(r0 + g * sub + 8, 8), :], g_ref, bt_ref)
            v = jnp.concatenate([lo, hi], axis=0).astype(xn_ref.dtype)
            stage_ref[rows, :] = v
            xn_ref[rows, :] = v

    _by_parity(m, body, xa_ref, xb_ref)


def _ln_proj(x2d, g, bt, w, b, tm, tn):
    T, D = x2d.shape
    N = w.shape[1]
    nm = T // tm
    last = lambda m, n: (jnp.minimum(m, nm - 1), 0)
    const = lambda m, n: (0, 0)
    return pl.pallas_call(
        functools.partial(_ln_proj_kernel, nn=N // tn), name="ln_proj_conv",
        grid=(nm + 1, N // tn),
        in_specs=[pl.BlockSpec((tm, D), last),
                  pl.BlockSpec((1, D), const), pl.BlockSpec((1, D), const),
                  pl.BlockSpec((D, tn), lambda m, n: (0, n)),
                  pl.BlockSpec((1, tn), lambda m, n: (0, n))],
        out_specs=[pl.BlockSpec((tm, D), last),
                   pl.BlockSpec((tm, tn), lambda m, n: (m, n))],
        out_shape=[jax.ShapeDtypeStruct((T, D), BF16),
                   jax.ShapeDtypeStruct(((nm + 1) * tm, N), BF16)],
        scratch_shapes=[pltpu.VMEM((tm, D), BF16), pltpu.VMEM((tm, D), BF16)],
        compiler_params=pltpu.CompilerParams(
            dimension_semantics=("arbitrary", "arbitrary"),
            vmem_limit_bytes=_vmem_limit(2 * (tm * D * 4 + tm * D * 2 + D * tn * 2 + tm * tn * 2) + 2 * tm * D * 2)),
    )(x2d, g, bt, w, b)


def _proj_heads_kernel(x_ref, w_ref, b_ref, *rest, dil, q_blocks, q_scale, has_rider):
    if has_rider:
        ri_ref, o_ref, ro_ref, acc_ref = rest
        ro_ref[...] = ri_ref[...].astype(ro_ref.dtype)
    else:
        o_ref, acc_ref = rest
    n = pl.program_id(2)
    nj, tm, _ = acc_ref.shape
    tn = nj * LANES
    scale = jnp.where(n < q_blocks, q_scale, 1.0).astype(F32)
    acc = (jnp.dot(x_ref[...], w_ref[...], preferred_element_type=F32) + b_ref[...]) * scale
    if dil == 1:
        for j in range(nj):
            o_ref[j, 0, 0] = acc[:, j * LANES:(j + 1) * LANES].astype(o_ref.dtype)
    else:
        for j in range(nj):
            acc_ref[j] = acc[:, j * LANES:(j + 1) * LANES]
        for j in range(nj):
            for r in range(dil):
                o_ref[j, 0, r] = acc_ref[j, pl.ds(r, tm // dil, stride=dil), :].astype(o_ref.dtype)


def _proj_heads(xn, w, b, B, S, dil, group_width, tm, tn, rider=None):
    T, D = xn.shape
    N = w.shape[1]
    L = S // dil
    mt = S // tm
    nn = N // tn
    kern = functools.partial(_proj_heads_kernel, dil=dil, q_blocks=group_width // tn,
                             q_scale=HEAD_DIM ** -0.5, has_rider=rider is not None)
    in_specs = [pl.BlockSpec((tm, D), lambda bb, m, n: (bb * mt + m, 0)),
                pl.BlockSpec((D, tn), lambda bb, m, n: (0, n)),
                pl.BlockSpec((1, tn), lambda bb, m, n: (0, n))]
    out_specs = [pl.BlockSpec((tn // LANES, 1, dil, tm // dil, LANES), lambda bb, m, n: (n, bb, 0, m, 0))]
    out_shape = [jax.ShapeDtypeStruct((N // LANES, B, dil, L, LANES), BF16)]
    args = [xn, w, b]
    nbytes = 2 * (tm * D * 2 + D * tn * 2 + tm * tn * 2) + 2 * tm * tn * 4
    if rider is not None:
        w2d, cblk, fc = rider
        rows = w2d.shape[0]
        steps = B * mt * nn
        nblk = 1
        while nblk * 2 <= steps and rows % (nblk * 2) == 0:
            nblk *= 2
        rb = rows // nblk
        rmap = lambda bb, m, n: (jnp.minimum((bb * mt + m) * nn + n, nblk - 1), cblk)
        in_specs.append(pl.BlockSpec((rb, fc), rmap))
        out_specs.append(pl.BlockSpec((rb, fc), lambda bb, m, n: (rmap(bb, m, n)[0], 0)))
        out_shape.append(jax.ShapeDtypeStruct((rows, fc), BF16))
        args.append(w2d)
        nbytes += 2 * rb * fc * 6
    outs = pl.pallas_call(
        kern, name=f"proj_attn_dil{dil}",
        grid=(B, mt, nn),
        in_specs=in_specs,
        out_specs=out_specs,
        out_shape=out_shape,
        scratch_shapes=[pltpu.VMEM((tn // LANES, tm, LANES), F32)],
        compiler_params=pltpu.CompilerParams(
            dimension_semantics=("arbitrary", "arbitrary", "arbitrary"),
            vmem_limit_bytes=_vmem_limit(nbytes)),
    )(*args)
    return outs if rider is not None else outs[0]


def _conv_kernel(cur_ref, prev_ref, next_ref, w_ref, cb_ref, g_ref, b_ref, o_ref, ext_ref, y_ref, sh_ref, *, C):
    i = pl.program_id(1)
    last = pl.num_programs(1) - 1
    tp = cur_ref.shape[0]

    def glu(blk):
        a = blk[:, :C].astype(F32)
        gate = blk[:, C:].astype(F32)
        return a * (1.0 / (1.0 + jnp.exp(-gate)))

    ext_ref[0:CONV_HALO] = glu(prev_ref[...]) * jnp.where(i > 0, 1.0, 0.0)
    ext_ref[CONV_HALO:CONV_HALO + tp] = glu(cur_ref[...])
    ext_ref[CONV_HALO + tp:] = glu(next_ref[...]) * jnp.where(i < last, 1.0, 0.0)

    first = CONV_HALO - CONV_WIDTH // 2

    sub = 8
    span = sh_ref.shape[1]

    def chan_block(cb, carry):
        c0 = pl.multiple_of(cb * LANES, LANES)
        for b in range(1, sub):
            sh_ref[b] = ext_ref[pl.ds(b, span), pl.ds(c0, LANES)]
        acc = jnp.zeros((tp, LANES), F32)
        for d in range(CONV_WIDTH):
            q = first + d
            a8, b = (q // sub) * sub, q % sub
            if b == 0:
                win = ext_ref[pl.ds(a8, tp), pl.ds(c0, LANES)]
            else:
                win = sh_ref[b, pl.ds(a8, tp), :]
            acc = acc + win * w_ref[pl.ds(d, 1), pl.ds(c0, LANES)]
        y_ref[:, pl.ds(c0, LANES)] = acc + cb_ref[:, pl.ds(c0, LANES)]
        return carry

    lax.fori_loop(0, C // LANES, chan_block, 0)
    y = _ln(y_ref[...], g_ref[...], b_ref[...])
    o_ref[0] = (y * (1.0 / (1.0 + jnp.exp(-y)))).astype(o_ref.dtype)


def _conv_module(ag, row_offset, conv_w, conv_b, ln_g, ln_b, B, S, tp):
    C = ag.shape[1] // 2
    hb = tp // CONV_HALO
    nh = S // CONV_HALO
    ob, oh = row_offset // tp, row_offset // CONV_HALO
    sb, sh = S // tp, S // CONV_HALO
    out = pl.pallas_call(
        functools.partial(_conv_kernel, C=C), name="conv_module",
        grid=(B, S // tp),
        in_specs=[pl.BlockSpec((tp, 2 * C), lambda b, i: (ob + b * sb + i, 0)),
                  pl.BlockSpec((CONV_HALO, 2 * C), lambda b, i: (oh + b * sh + jnp.maximum(i * hb - 1, 0), 0)),
                  pl.BlockSpec((CONV_HALO, 2 * C),
                               lambda b, i: (oh + b * sh + jnp.minimum((i + 1) * hb, nh - 1), 0)),
                  pl.BlockSpec((CONV_WIDTH, C), lambda b, i: (0, 0)),
                  pl.BlockSpec((1, C), lambda b, i: (0, 0)),
                  pl.BlockSpec((1, C), lambda b, i: (0, 0)),
                  pl.BlockSpec((1, C), lambda b, i: (0, 0))],
        out_specs=pl.BlockSpec((1, tp, C), lambda b, i: (b, i, 0)),
        out_shape=jax.ShapeDtypeStruct((B, S, C), BF16),
        scratch_shapes=[pltpu.VMEM((tp + 2 * CONV_HALO, C), F32), pltpu.VMEM((tp, C), F32),
                        pltpu.VMEM((8, tp + 2 * CONV_HALO - 8, LANES), F32)],
        compiler_params=pltpu.CompilerParams(
            dimension_semantics=("parallel", "parallel"),
            vmem_limit_bytes=_vmem_limit(2 * tp * 2 * C * 2 + 2 * tp * C * 2 + 3 * tp * C * 4)),
    )(ag, ag, ag, conv_w, conv_b, ln_g, ln_b)
    return out.reshape(B * S, C)


def _attn_kernel(slopes_ref, *refs, tp, dils, heads):
    ng = len(dils)
    ins = refs[:7 * ng]
    o_ref = refs[7 * ng]
    scr = refs[7 * ng + 1:]
    kwins, vwins = scr[:ng], scr[ng:2 * ng]
    o_nat, m_nat, l_nat = scr[2 * ng:2 * ng + 3]

    h = pl.program_id(1)
    i = pl.program_id(2)
    W = HALF_WINDOW

    for g, dil in enumerate(dils):
        q_ref, kc, kp, kn, vc, vp, vn = ins[7 * g:7 * g + 7]
        kwin, vwin = kwins[g], vwins[g]
        n = tp // dil
        L = n * pl.num_programs(2)
        sq = min(128, n)
        nsub = n // sq
        for win, cur, prev, nxt in ((kwin, kc, kp, kn), (vwin, vc, vp, vn)):
            win[:, 0:W] = prev[0, 0]
            win[:, W:W + n] = cur[0, 0]
            win[:, W + n:] = nxt[0, 0]

        slope = slopes_ref[g, h] * float(dil)
        row = lax.broadcasted_iota(jnp.int32, (sq, sq + 2 * W), 0)
        col = lax.broadcasted_iota(jnp.int32, (sq, sq + 2 * W), 1)
        dist = jnp.abs(col - W - row)
        bias = jnp.where(dist <= W, -slope * dist.astype(F32), NEG_INF)
        col1 = lax.broadcasted_iota(jnp.int32, (1, sq + 2 * W), 1)

        for r in range(dil):
            for sub in range(nsub):
                qs = q_ref[0, 0, r, sub * sq:(sub + 1) * sq, :]
                kw = kwin[r, sub * sq:sub * sq + sq + 2 * W, :]
                vw = vwin[r, sub * sq:sub * sq + sq + 2 * W, :]
                s = lax.dot_general(qs, kw, (((1,), (1,)), ((), ())), preferred_element_type=F32) + bias
                if sub == 0 or sub == nsub - 1:
                    kidx = i * n + (sub * sq - W) + col1
                    s = s + jnp.where((kidx >= 0) & (kidx < L), 0.0, NEG_INF)
                m = jnp.max(s, axis=-1, keepdims=True)
                p = jnp.exp(s - m)
                l = jnp.sum(p, axis=-1, keepdims=True)
                o = jnp.dot(p.astype(BF16), vw, preferred_element_type=F32)
                if dil == 1:
                    rows = pl.ds(sub * sq, sq)
                else:
                    rows = pl.ds(sub * sq * dil + r, sq, stride=dil)
                o_nat[g, rows, :] = o
                m_nat[g, rows, :] = jnp.broadcast_to(m, (sq, LANES))
                l_nat[g, rows, :] = jnp.broadcast_to(l, (sq, LANES))

    ch = min(128, tp)

    def merge(c, carry):
        r0 = pl.multiple_of(c * ch, ch)
        ms = [m_nat[g, pl.ds(r0, ch), :] for g in range(ng)]
        mx = functools.reduce(jnp.maximum, ms)
        num = jnp.zeros((ch, LANES), F32)
        den = jnp.zeros((ch, LANES), F32)
        for g in range(ng):
            wg = jnp.exp(ms[g] - mx)
            num = num + wg * o_nat[g, pl.ds(r0, ch), :]
            den = den + wg * l_nat[g, pl.ds(r0, ch), :]
        o_ref[0, pl.ds(r0, ch), :] = (num / den).astype(o_ref.dtype)
        return carry

    lax.fori_loop(0, tp // ch, merge, 0)


def _attention(qkvs, slopes, B, S, heads, tp):
    dils = tuple(d for _, d in ATT_GROUPS)
    W = HALF_WINDOW
    H = heads
    in_specs = [pl.BlockSpec(memory_space=pltpu.SMEM)]
    args = [slopes]
    scratch_k, scratch_v = [], []
    nbytes = 0
    for g, dil in enumerate(dils):
        n = tp // dil
        L = S // dil
        hb = n // W
        nh = L // W

        def cur_map(which):
            return lambda b, h, i: (which * H + h, b, 0, i, 0)

        def prev_map(which, hb=hb):
            return lambda b, h, i: (which * H + h, b, 0, jnp.maximum(i * hb - 1, 0), 0)

        def next_map(which, hb=hb, nh=nh):
            return lambda b, h, i: (which * H + h, b, 0, jnp.minimum((i + 1) * hb, nh - 1), 0)

        blk = (1, 1, dil, n, LANES)
        halo = (1, 1, dil, W, LANES)
        in_specs += [pl.BlockSpec(blk, cur_map(0)),
                     pl.BlockSpec(blk, cur_map(1)), pl.BlockSpec(halo, prev_map(1)), pl.BlockSpec(halo, next_map(1)),
                     pl.BlockSpec(blk, cur_map(2)), pl.BlockSpec(halo, prev_map(2)), pl.BlockSpec(halo, next_map(2))]
        args += [qkvs[g]] * 7
        scratch_k.append(pltpu.VMEM((dil, n + 2 * W, LANES), BF16))
        scratch_v.append(pltpu.VMEM((dil, n + 2 * W, LANES), BF16))
        nbytes += 2 * (3 * tp + 4 * dil * W) * LANES * 2 + 2 * (tp + 2 * W * dil) * LANES * 2
    nbytes += 9 * tp * LANES * 4 + 2 * tp * LANES * 2
    ng = len(dils)
    out = pl.pallas_call(
        functools.partial(_attn_kernel, tp=tp, dils=dils, heads=H), name="attention",
        grid=(B, H, S // tp),
        in_specs=in_specs,
        out_specs=pl.BlockSpec((1, tp, LANES), lambda b, h, i: (b, i, h)),
        out_shape=jax.ShapeDtypeStruct((B, S, H * LANES), BF16),
        scratch_shapes=scratch_k + scratch_v + [pltpu.VMEM((ng, tp, LANES), F32)] * 3,
        compiler_params=pltpu.CompilerParams(
            dimension_semantics=("parallel", "parallel", "parallel"),
            vmem_limit_bytes=_vmem_limit(nbytes)),
    )(*args)
    return out.reshape(B * S, H * LANES)


def _outproj_kernel(*refs, alpha, tile_starts, tn):
    nk = len(tile_starts) - 1
    trunk_refs = [refs[3 * k:3 * k + 3] for k in range(nk)]
    (lig_ref, lib_ref, wu_ref, wo_ref, bo_ref, g1_ref, b1_ref, x1_ref) = refs[3 * nk:3 * nk + 8]
    set_a, set_b = refs[3 * nk + 8:3 * nk + 11], refs[3 * nk + 11:]
    m = pl.program_id(0)
    n = pl.program_id(1)
    tm = set_a[0].shape[0]
    chunk = tm // (set_a[0].shape[1] // tn)
    sub = 8

    @pl.when((m == 0) & (n == 0))
    def _():
        for ref in set_a + set_b:
            ref[...] = jnp.zeros_like(ref)

    def body(x_ref, u_ref, o_ref, stage, work):
        acc_s, us_s, os_s = stage
        acc_w, us_w, os_w = work
        c0 = pl.multiple_of(n * tn, LANES)
        mix = (jnp.dot(us_w[...], wu_ref[:, pl.ds(c0, tn)], preferred_element_type=F32)
               + jnp.dot(os_w[...], wo_ref[:, pl.ds(c0, tn)], preferred_element_type=F32)
               + bo_ref[:, pl.ds(c0, tn)])
        acc_w[:, pl.ds(c0, tn)] = acc_w[:, pl.ds(c0, tn)] + mix
        r0 = pl.multiple_of(n * chunk, chunk)
        for g in range(chunk // sub):
            rows = pl.ds(r0 + g * sub, sub)
            x1_ref[rows, :] = _ln_rows(acc_s[rows, :], g1_ref, b1_ref)
            acc_s[rows, :] = alpha * _ln_rows(x_ref[rows, :], lig_ref, lib_ref)
        us_s[pl.ds(r0, chunk), :] = u_ref[pl.ds(r0, chunk), :]
        os_s[pl.ds(r0, chunk), :] = o_ref[pl.ds(r0, chunk), :]

    for k, (x_ref, u_ref, o_ref) in enumerate(trunk_refs):
        lo = tile_starts[k]
        cond = (m >= lo) if k == nk - 1 else ((m >= lo) & (m < tile_starts[k + 1]))
        pl.when(cond)(functools.partial(_by_parity, m, functools.partial(body, x_ref, u_ref, o_ref),
                                        set_a, set_b))


def _outproj(trunk_inputs, ln_in_g, ln_in_b, wu, wo, b_out, ln1_g, ln1_b, alpha, tm, tn):
    D = trunk_inputs[0][0].shape[1]
    C = wu.shape[0]
    GW = wo.shape[0]
    tile_starts = [0]
    for x, _, _ in trunk_inputs:
        tile_starts.append(tile_starts[-1] + x.shape[0] // tm)
    nm = tile_starts[-1]
    const = lambda m, n: (0, 0)
    in_specs, args = [], []
    for k, (x, u, o) in enumerate(trunk_inputs):
        def rows(m, n, lo=tile_starts[k], cnt=tile_starts[k + 1] - tile_starts[k]):
            return (jnp.clip(m - lo, 0, cnt - 1), 0)
        in_specs += [pl.BlockSpec((tm, D), rows), pl.BlockSpec((tm, C), rows), pl.BlockSpec((tm, GW), rows)]
        args += [x, u, o]
    in_specs += [pl.BlockSpec((1, D), const), pl.BlockSpec((1, D), const),
                 pl.BlockSpec((C, D), const, pipeline_mode=pl.Buffered(1)),
                 pl.BlockSpec((GW, D), const, pipeline_mode=pl.Buffered(1)),
                 pl.BlockSpec((1, D), const),
                 pl.BlockSpec((1, D), const), pl.BlockSpec((1, D), const)]
    args += [ln_in_g, ln_in_b, wu, wo, b_out, ln1_g, ln1_b]
    nk = len(trunk_inputs)
    nbytes = (2 * (nk * (tm * D * 4 + tm * (C + GW) * 2) + tm * D * 4) + (C + GW) * D * 2
              + 2 * tm * D * 4 + 2 * tm * (C + GW) * 2)
    return pl.pallas_call(
        functools.partial(_outproj_kernel, alpha=alpha, tile_starts=tuple(tile_starts), tn=tn), name="outproj",
        grid=(nm + 2, D // tn),
        in_specs=in_specs,
        out_specs=pl.BlockSpec((tm, D), lambda m, n: (jnp.clip(m - 2, 0, nm - 1), 0)),
        out_shape=jax.ShapeDtypeStruct((nm * tm, D), F32),
        scratch_shapes=[pltpu.VMEM((tm, D), F32), pltpu.VMEM((tm, C), BF16), pltpu.VMEM((tm, GW), BF16)] * 2,
        compiler_params=pltpu.CompilerParams(
            dimension_semantics=("arbitrary", "arbitrary"),
            vmem_limit_bytes=_vmem_limit(nbytes)),
    )(*args)


def _router_kernel(x_ref, wh_ref, whl_ref, b_ref, gate_ref, idx_ref, *, n_groups, per):
    x = x_ref[...]
    xh = x.astype(BF16)
    xl = (x - xh.astype(F32)).astype(BF16)
    rl = wh_ref.shape[1]
    both = jnp.dot(xh, whl_ref[...], preferred_element_type=F32)
    lg = both[:, :rl] + both[:, rl:] + jnp.dot(xl, wh_ref[...], preferred_element_type=F32) + b_ref[...]
    lane = lax.broadcasted_iota(jnp.int32, lg.shape, 1)
    big = jnp.int32(lg.shape[1])

    def top1(v):
        mx = jnp.max(v, axis=-1, keepdims=True)
        return mx, jnp.min(jnp.where(v == mx, lane, big), axis=-1, keepdims=True)

    gl = jnp.where(lane < n_groups, lg, NEG_INF)
    gmax, g_star = top1(gl)
    p_group = 1.0 / jnp.sum(jnp.exp(gl - gmax), axis=-1, keepdims=True)
    first = n_groups + g_star * per
    el = jnp.where((lane >= first) & (lane < first + per), lg, NEG_INF)
    v0, i0 = top1(el)
    v1, i1 = top1(jnp.where(lane == i0, NEG_INF, el))
    e1 = jnp.exp(v1 - v0)
    g0 = p_group / (1.0 + e1)
    g1 = g0 * e1
    gate_ref[...] = jnp.where(lane == 0, g0, jnp.where(lane == 1, g1, 0.0))
    idx_ref[...] = jnp.where(lane == 0, i0 - n_groups, jnp.where(lane == 1, i1 - n_groups, 0))


def _router(x1, wrh, wrl, br, n_groups, per, tm):
    T, D = x1.shape
    RL = wrh.shape[1]
    const = lambda i: (0, 0)
    return pl.pallas_call(
        functools.partial(_router_kernel, n_groups=n_groups, per=per), name="router",
        grid=(T // tm,),
        in_specs=[pl.BlockSpec((tm, D), lambda i: (i, 0)),
                  pl.BlockSpec((D, RL), const), pl.BlockSpec((D, 2 * RL), const), pl.BlockSpec((1, RL), const)],
        out_specs=[pl.BlockSpec((tm, RL), lambda i: (i, 0)), pl.BlockSpec((tm, RL), lambda i: (i, 0))],
        out_shape=[jax.ShapeDtypeStruct((T, RL), F32), jax.ShapeDtypeStruct((T, RL), jnp.int32)],
        compiler_params=pltpu.CompilerParams(
            dimension_semantics=("parallel",),
            vmem_limit_bytes=_vmem_limit(2 * tm * D * 4 + tm * D * 4 + 4 * D * RL * 2 + 4 * tm * RL * 4)),
    )(x1, wrh, jnp.concatenate([wrh, wrl], axis=1), br)


def _start_row_gather(src_hbm, idx_ref, dst, sem, n_rows):
    for r in range(n_rows):
        pltpu.make_async_copy(src_hbm.at[pl.ds(idx_ref[0, 0, r], 1)], dst.at[pl.ds(r, 1)], sem).start()


def _wait_row_gather(src_hbm, dst, sem, n_rows):
    for r in range(n_rows):
        pltpu.make_async_copy(src_hbm.at[pl.ds(0, 1)], dst.at[pl.ds(r, 1)], sem).wait()


def _expert_changed(te_ref, t):
    return (t == 0) | (te_ref[t] != te_ref[jnp.maximum(t - 1, 0)])


def _moe_up_kernel(te_ref, tv_ref, idx_ref, idxn_ref, x_hbm, *rest, n_pieces):
    wg_refs, wu_refs = rest[:n_pieces], rest[n_pieces:2 * n_pieces]
    h_ref, xbuf, sem = rest[2 * n_pieces:]
    t = pl.program_id(0)
    nt = pl.num_programs(0)
    tm = xbuf.shape[1]
    slot = t % 2

    @pl.when((t == 0) & (tv_ref[0] > 0))
    def _():
        _start_row_gather(x_hbm, idx_ref, xbuf.at[0], sem.at[0], tm)

    nxt = jnp.minimum(t + 1, nt - 1)

    @pl.when((t + 1 < nt) & (tv_ref[nxt] > 0))
    def _():
        _start_row_gather(x_hbm, idxn_ref, xbuf.at[1 - slot], sem.at[1 - slot], tm)

    @pl.when(tv_ref[t] > 0)
    def _():
        _wait_row_gather(x_hbm, xbuf.at[slot], sem.at[slot], tm)
        xb = xbuf[slot].astype(BF16)
        fc = h_ref.shape[1] // n_pieces
        for p in range(n_pieces):
            a = jnp.dot(xb, wg_refs[p][0], preferred_element_type=F32)
            b = jnp.dot(xb, wu_refs[p][0], preferred_element_type=F32)
            h_ref[:, p * fc:(p + 1) * fc] = (a * (1.0 / (1.0 + jnp.exp(-a))) * b).astype(h_ref.dtype)

    @pl.when(tv_ref[t] == 0)
    def _():
        h_ref[...] = jnp.zeros_like(h_ref)


def _moe_up(tile_e, tile_valid, buf_tok3, x1, wg_pieces, wu_pieces, tm):
    nt = tile_e.shape[0]
    n_pieces = len(wg_pieces)
    E, D, fc = wg_pieces[0].shape
    F = fc * n_pieces
    wspec = pl.BlockSpec((1, D, fc), lambda t, te, tv: (te[t], 0, 0))
    grid_spec = pltpu.PrefetchScalarGridSpec(
        num_scalar_prefetch=2,
        grid=(nt,),
        in_specs=[pl.BlockSpec((1, 1, tm), lambda t, te, tv: (t, 0, 0), memory_space=pltpu.SMEM),
                  pl.BlockSpec((1, 1, tm), lambda t, te, tv: (jnp.minimum(t + 1, nt - 1), 0, 0),
                               memory_space=pltpu.SMEM),
                  pl.BlockSpec(memory_space=pl.ANY)] + [wspec] * (2 * n_pieces),
        out_specs=pl.BlockSpec((tm, F), lambda t, te, tv: (t, 0)),
        scratch_shapes=[pltpu.VMEM((2, tm, D), F32), pltpu.SemaphoreType.DMA((2,))])
    return pl.pallas_call(
        functools.partial(_moe_up_kernel, n_pieces=n_pieces), name="moe_up",
        grid_spec=grid_spec,
        out_shape=jax.ShapeDtypeStruct((nt * tm, F), BF16),
        compiler_params=pltpu.CompilerParams(
            dimension_semantics=("arbitrary",),
            vmem_limit_bytes=_vmem_limit(4 * D * F * 2 + 2 * tm * D * 4 + 2 * tm * F * 2 + tm * D * 2)),
    )(tile_e, tile_valid, buf_tok3, buf_tok3, x1, *wg_pieces, *wu_pieces)


def _bf16_bits(v):
    u = pltpu.bitcast(v, jnp.uint32)
    return u + jnp.uint32(0x7FFF) + ((u >> 16) & jnp.uint32(1))


def _moe_down_kernel(te_ref, tv_ref, h_ref, wd_ref, y_ref, wdb_ref):
    t = pl.program_id(0)
    valid = tv_ref[t] > 0
    half = y_ref.shape[1]

    @pl.when(valid & _expert_changed(te_ref, t))
    def _():
        wdb_ref[...] = wd_ref[0].astype(BF16)

    @pl.when(valid)
    def _():
        y = jnp.dot(h_ref[...], wdb_ref[...], preferred_element_type=F32)
        lo = _bf16_bits(y[:, :half]) >> 16
        hi = _bf16_bits(y[:, half:]) & jnp.uint32(0xFFFF0000)
        y_ref[...] = lo | hi

    @pl.when(jnp.logical_not(valid))
    def _():
        y_ref[...] = jnp.zeros_like(y_ref)


def _moe_down(tile_e, tile_valid, hmid, wd, tm):
    nt = tile_e.shape[0]
    E, F, D = wd.shape
    grid_spec = pltpu.PrefetchScalarGridSpec(
        num_scalar_prefetch=2,
        grid=(nt,),
        in_specs=[pl.BlockSpec((tm, F), lambda t, te, tv: (t, 0)),
                  pl.BlockSpec((1, F, D), lambda t, te, tv: (te[t], 0, 0))],
        out_specs=pl.BlockSpec((tm, D // 2), lambda t, te, tv: (t, 0)),
        scratch_shapes=[pltpu.VMEM((F, D), BF16)])
    return pl.pallas_call(
        _moe_down_kernel, name="moe_down",
        grid_spec=grid_spec,
        out_shape=jax.ShapeDtypeStruct((nt * tm, D // 2), jnp.uint32),
        compiler_params=pltpu.CompilerParams(
            dimension_semantics=("arbitrary",),
            vmem_limit_bytes=_vmem_limit(2 * (F * D * 4 + tm * F * 2 + tm * D * 2) + F * D * 2 + tm * D * 4)),
    )(tile_e, tile_valid, hmid, wd)


def _combine_kernel(idx_ref, idxn_ref, x1_ref, gate_ref, g_ref, b_ref, ys_hbm, o_ref, ybuf, sem, *, alpha):
    t = pl.program_id(0)
    nt = pl.num_programs(0)
    tm, D = x1_ref.shape
    half = D // 2
    slot = t % 2

    @pl.when(t == 0)
    def _():
        _start_row_gather(ys_hbm, idx_ref, ybuf.at[0], sem.at[0], TOP_K * tm)

    @pl.when(t + 1 < nt)
    def _():
        _start_row_gather(ys_hbm, idxn_ref, ybuf.at[1 - slot], sem.at[1 - slot], TOP_K * tm)

    _wait_row_gather(ys_hbm, ybuf.at[slot], sem.at[slot], TOP_K * tm)
    gates = gate_ref[...]
    w0 = ybuf[slot, 0:tm, :]
    w1 = ybuf[slot, tm:2 * tm, :]
    unpack_lo = lambda w: pltpu.bitcast(w << 16, F32)
    unpack_hi = lambda w: pltpu.bitcast(w & jnp.uint32(0xFFFF0000), F32)
    g0, g1 = gates[:, 0:1], gates[:, 1:2]
    r_lo = alpha * x1_ref[:, :half] + g0 * unpack_lo(w0) + g1 * unpack_lo(w1)
    r_hi = alpha * x1_ref[:, half:] + g0 * unpack_hi(w0) + g1 * unpack_hi(w1)
    mu = (jnp.sum(r_lo, axis=-1, keepdims=True) + jnp.sum(r_hi, axis=-1, keepdims=True)) / D
    c_lo = r_lo - mu
    c_hi = r_hi - mu
    var = (jnp.sum(c_lo * c_lo, axis=-1, keepdims=True) + jnp.sum(c_hi * c_hi, axis=-1, keepdims=True)) / D
    rstd = lax.rsqrt(var + LN_EPS)
    o_ref[:, :half] = c_lo * rstd * g_ref[:, :half] + b_ref[:, :half]
    o_ref[:, half:] = c_hi * rstd * g_ref[:, half:] + b_ref[:, half:]


def _combine(dest3, x1, row_offset, T, gates, ln_g, ln_b, ys, alpha, tm):
    D = x1.shape[1]
    nt = T // tm
    mo = row_offset // tm
    return pl.pallas_call(
        functools.partial(_combine_kernel, alpha=alpha), name="moe_combine",
        grid=(nt,),
        in_specs=[pl.BlockSpec((1, 1, TOP_K * tm), lambda t: (mo + t, 0, 0), memory_space=pltpu.SMEM),
                  pl.BlockSpec((1, 1, TOP_K * tm), lambda t: (mo + jnp.minimum(t + 1, nt - 1), 0, 0),
                               memory_space=pltpu.SMEM),
                  pl.BlockSpec((tm, D), lambda t: (mo + t, 0)),
                  pl.BlockSpec((tm, TOP_K), lambda t: (mo + t, 0)),
                  pl.BlockSpec((1, D), lambda t: (0, 0)),
                  pl.BlockSpec((1, D), lambda t: (0, 0)),
                  pl.BlockSpec(memory_space=pl.ANY)],
        out_specs=pl.BlockSpec((tm, D), lambda t: (t, 0)),
        out_shape=jax.ShapeDtypeStruct((T, D), F32),
        scratch_shapes=[pltpu.VMEM((2, TOP_K * tm, D // 2), jnp.uint32), pltpu.SemaphoreType.DMA((2,))],
        compiler_params=pltpu.CompilerParams(
            dimension_semantics=("arbitrary",),
            vmem_limit_bytes=_vmem_limit(4 * tm * D * 4 + 2 * TOP_K * tm * D * 2 + 4 * tm * D * 4)),
    )(dest3, dest3, x1, gates, ln_g, ln_b, ys)


def _dispatch_plan(flat_e, n_experts, tile):
    A = flat_e.shape[0]
    onehot = (flat_e[:, None] == jnp.arange(n_experts, dtype=jnp.int32)[None, :])
    blk = 256
    oh3 = onehot.astype(BF16).reshape(A // blk, blk, n_experts)
    tri = (jnp.arange(blk)[:, None] > jnp.arange(blk)[None, :]).astype(BF16)
    local = jnp.einsum('ij,bjk->bik', tri, oh3, preferred_element_type=F32)
    blk_tot = jnp.sum(oh3.astype(F32), axis=1)
    blk_off = jnp.cumsum(blk_tot, axis=0) - blk_tot
    excl = (local + blk_off[:, None, :]).reshape(A, n_experts)
    counts = jnp.sum(blk_tot, axis=0).astype(jnp.int32)
    rank = jnp.sum(jnp.where(onehot, excl, 0.0), axis=1).astype(jnp.int32)
    padded = (counts + tile - 1) // tile * tile
    pend = jnp.cumsum(padded)
    pstart = pend - padded
    dest = (pstart[flat_e] + rank).astype(jnp.int32)
    n_tiles = A // tile + n_experts
    flat_tok = jnp.arange(A, dtype=jnp.int32) // TOP_K
    buf_tok = jnp.zeros((n_tiles * tile,), jnp.int32).at[dest].set(flat_tok, unique_indices=True)
    tile_start = jnp.arange(n_tiles, dtype=jnp.int32) * tile
    tile_e = jnp.minimum(jnp.sum((pend[None, :] <= tile_start[:, None]).astype(jnp.int32), axis=1), n_experts - 1)
    tile_valid = (tile_start < pend[-1]).astype(jnp.int32)
    return dest, buf_tok, tile_e, tile_valid


def _tiles(D, S, C, GW, F):
    max_dil = max(d for _, d in ATT_GROUPS)
    t = {}
    t['ln_rows'] = min(512, S)
    t['proj_rows'] = min(1024, S)
    t['proj_cols'] = min(512, GW)
    t['conv_rows'] = min(256, S)
    t['attn_rows'] = min(2048, S)
    t['out_rows'] = min(256, S)
    t['out_cols'] = min(1024, D)
    t['moe_rows'] = min(256, S)
    t['router_rows'] = min(512, S)
    assert t['proj_rows'] % (16 * max_dil) == 0 and t['attn_rows'] % (HALF_WINDOW * max_dil) == 0
    assert S % t['proj_rows'] == 0 and S % t['attn_rows'] == 0 and S % t['conv_rows'] == 0
    return t


def kernel(x_prompt, x_sample, ln_in_g, ln_in_b, w_in, b_in, conv_w, conv_b, conv_ln_g, conv_ln_b, w_out, b_out,
           ln1_g, ln1_b, w_router_group, b_router_group, w_router_expert, b_router_expert, w_gate, w_up, w_down,
           ln2_g, ln2_b):
    depth = w_in.shape[0]
    assert depth == 1
    assert all(w // (2 * d) == HALF_WINDOW for w, d in ATT_GROUPS)
    alpha = (2.0 * depth) ** 0.25
    D = x_prompt.shape[-1]
    C = conv_w.shape[-1]
    n_att = len(ATT_GROUPS)
    GW = (w_in.shape[-1] - 2 * C) // (3 * n_att)
    H = GW // HEAD_DIM
    n_groups = w_router_group.shape[-1]
    n_experts = w_router_expert.shape[-1]
    F = w_gate.shape[-1]
    row = lambda v: v.reshape(1, -1).astype(F32)

    w_in_b = w_in[0].astype(BF16)
    b_in_r = row(b_in[0])
    w_sections = [(w_in_b[:, :2 * C], b_in_r[:, :2 * C])]
    off = 2 * C
    for _ in range(n_att):
        w_sections.append((w_in_b[:, off:off + 3 * GW], b_in_r[:, off:off + 3 * GW]))
        off += 3 * GW
    w_out_b = w_out[0].astype(BF16)
    wu_out, wo_out = w_out_b[:C], w_out_b[C:]
    RL = LANES
    w_r = jnp.concatenate([w_router_group[0], w_router_expert[0]], axis=1).astype(F32)
    w_r = jnp.pad(w_r, ((0, 0), (0, RL - w_r.shape[1])))
    wrh = w_r.astype(BF16)
    wrl = (w_r - wrh.astype(F32)).astype(BF16)
    b_r = jnp.pad(jnp.concatenate([b_router_group[0], b_router_expert[0]]).astype(F32),
                  (0, RL - n_groups - n_experts)).reshape(1, RL)
    n_heads = n_att * H
    slopes = jnp.exp2(-8.0 * jnp.arange(1, n_heads + 1, dtype=F32) / n_heads).reshape(n_att, H)

    n_fp = 2 if F % (2 * LANES) == 0 else 1
    fc = F // n_fp
    riders = [(wmat[0].reshape(n_experts * D, F), c, fc, (name, c))
              for c in range(n_fp) for name, wmat in (('gate', w_gate), ('up', w_up))]
    n_rider_hosts = (len(riders) + 1) // 2
    expert_bf16 = {}
    trunks = [x_prompt, x_sample]
    total = sum(x.shape[0] * x.shape[1] for x in trunks)
    staged = []
    offsets = []
    row_offset = 0
    for x in trunks:
        B, S, _ = x.shape
        T = B * S
        t = _tiles(D, S, C, GW, F)
        x2d = x.reshape(T, D)
        xn, ag = _ln_proj(x2d, row(ln_in_g), row(ln_in_b), w_sections[0][0], w_sections[0][1],
                          t['ln_rows'], min(t['proj_cols'], 2 * C))
        u = _conv_module(ag, t['ln_rows'], conv_w[0].astype(F32), row(conv_b[0]), row(conv_ln_g[0]),
                         row(conv_ln_b[0]), B, S, t['conv_rows'])
        qkvs = []
        for g, (_, dil) in enumerate(ATT_GROUPS):
            rider = riders.pop(0) if (riders and g < n_rider_hosts) else None
            res = _proj_heads(xn, w_sections[1 + g][0], w_sections[1 + g][1], B, S, dil, GW,
                              t['proj_rows'], t['proj_cols'], rider=None if rider is None else rider[:3])
            if rider is not None:
                expert_bf16[rider[3]] = res[1].reshape(n_experts, D, rider[2])
                res = res[0]
            qkvs.append(res)
        o = _attention(qkvs, slopes, B, S, H, t['attn_rows'])
        staged.append((x2d, u, o))
        offsets.append((row_offset, T, t))
        row_offset += T
    t0 = offsets[0][2]
    x1 = _outproj(staged, row(ln_in_g), row(ln_in_b), wu_out, wo_out, row(b_out[0]),
                  row(ln1_g[0]), row(ln1_b[0]), alpha, t0['out_rows'], t0['out_cols'])
    gate_l, idx_l = _router(x1, wrh, wrl, b_r, n_groups, n_experts // n_groups, t0['router_rows'])
    gates = gate_l[:, :TOP_K]

    tm = t0['moe_rows']
    dest, buf_tok, tile_e, tile_valid = _dispatch_plan(idx_l[:, :TOP_K].reshape(total * TOP_K), n_experts, tm)
    nt = tile_e.shape[0]
    hmid = _moe_up(tile_e, tile_valid, buf_tok.reshape(nt, 1, tm), x1,
                   [expert_bf16[('gate', c)] for c in range(n_fp)],
                   [expert_bf16[('up', c)] for c in range(n_fp)], tm)
    ys = _moe_down(tile_e, tile_valid, hmid, w_down[0], tm)
    dest3 = dest.reshape(total // tm, tm, TOP_K).transpose(0, 2, 1).reshape(total // tm, 1, TOP_K * tm)
    outs = []
    for x, (ro, T, t) in zip(trunks, offsets):
        y = _combine(dest3, x1, ro, T, gates, row(ln2_g[0]), row(ln2_b[0]), ys, alpha, tm)
        outs.append(y.reshape(x.shape))
    return tuple(outs)
```

```python
import functools

import jax
import jax.numpy as jnp
from jax import lax
from jax.experimental import pallas as pl
from jax.experimental.pallas import tpu as pltpu

F32 = jnp.float32
BF16 = jnp.bfloat16

HEAD_DIM = 128
ATT_GROUPS = ((128, 1), (512, 4), (2048, 16))
HALF_WINDOW = 64
CONV_WIDTH = 31
CONV_HALO = 16
TOP_K = 2
LN_EPS = 1e-5
NEG_INF = -1e30
LANES = 128
V7X_VMEM_BYTES = 64 * 1024 * 1024


def _vmem_limit(nbytes):
    return int(min(max(nbytes * 5 // 4 + (4 << 20), 16 << 20), V7X_VMEM_BYTES - (6 << 20)))


def _ln(x, g, b):
    mu = jnp.mean(x, axis=-1, keepdims=True)
    xc = x - mu
    var = jnp.mean(xc * xc, axis=-1, keepdims=True)
    return xc * lax.rsqrt(var + LN_EPS) * g + b


def _ln_cast_kernel(x_ref, g_ref, b_ref, o_ref):
    o_ref[...] = _ln(x_ref[...], g_ref[...], b_ref[...]).astype(o_ref.dtype)


def _ln_cast(x2d, g, b, tm):
    T, D = x2d.shape
    return pl.pallas_call(
        _ln_cast_kernel, name="ln_cast",
        grid=(T // tm,),
        in_specs=[pl.BlockSpec((tm, D), lambda i: (i, 0)),
                  pl.BlockSpec((1, D), lambda i: (0, 0)),
                  pl.BlockSpec((1, D), lambda i: (0, 0))],
        out_specs=pl.BlockSpec((tm, D), lambda i: (i, 0)),
        out_shape=jax.ShapeDtypeStruct((T, D), BF16),
        compiler_params=pltpu.CompilerParams(
            dimension_semantics=("parallel",),
            vmem_limit_bytes=_vmem_limit(2 * tm * D * 6)),
    )(x2d, g, b)


def _proj_nat_kernel(x_ref, w_ref, b_ref, o_ref):
    acc = jnp.dot(x_ref[...], w_ref[...], preferred_element_type=F32) + b_ref[...]
    o_ref[...] = acc.astype(o_ref.dtype)


def _proj_nat(xn, w, b, tm, tn):
    T, D = xn.shape
    N = w.shape[1]
    return pl.pallas_call(
        _proj_nat_kernel, name="proj_conv",
        grid=(T // tm, N // tn),
        in_specs=[pl.BlockSpec((tm, D), lambda m, n: (m, 0)),
                  pl.BlockSpec((D, tn), lambda m, n: (0, n)),
                  pl.BlockSpec((1, tn), lambda m, n: (0, n))],
        out_specs=pl.BlockSpec((tm, tn), lambda m, n: (m, n)),
        out_shape=jax.ShapeDtypeStruct((T, N), BF16),
        compiler_params=pltpu.CompilerParams(
            dimension_semantics=("parallel", "arbitrary"),
            vmem_limit_bytes=_vmem_limit(2 * (tm * D * 2 + D * tn * 2 + tm * tn * 2) + tm * tn * 4)),
    )(xn, w, b)


def _proj_heads_kernel(x_ref, w_ref, b_ref, *rest, dil, q_blocks, q_scale, has_rider):
    if has_rider:
        ri_ref, o_ref, ro_ref, acc_ref = rest
        ro_ref[...] = ri_ref[...].astype(ro_ref.dtype)
    else:
        o_ref, acc_ref = rest
    n = pl.program_id(2)
    nj, tm, _ = acc_ref.shape
    scale = jnp.where(n < q_blocks, q_scale, 1.0).astype(F32)
    acc = (jnp.dot(x_ref[...], w_ref[...], preferred_element_type=F32) + b_ref[...]) * scale
    if dil == 1:
        for j in range(nj):
            o_ref[j, 0, 0] = acc[:, j * LANES:(j + 1) * LANES].astype(o_ref.dtype)
    else:
        for j in range(nj):
            acc_ref[j] = acc[:, j * LANES:(j + 1) * LANES]
        for j in range(nj):
            for r in range(dil):
                o_ref[j, 0, r] = acc_ref[j, pl.ds(r, tm // dil, stride=dil), :].astype(o_ref.dtype)


def _proj_heads(xn, w, b, B, S, dil, group_width, tm, tn, rider=None):
    T, D = xn.shape
    N = w.shape[1]
    L = S // dil
    mt = S // tm
    nn = N // tn
    kern = functools.partial(_proj_heads_kernel, dil=dil, q_blocks=group_width // tn,
                             q_scale=HEAD_DIM ** -0.5, has_rider=rider is not None)
    in_specs = [pl.BlockSpec((tm, D), lambda bb, m, n: (bb * mt + m, 0)),
                pl.BlockSpec((D, tn), lambda bb, m, n: (0, n)),
                pl.BlockSpec((1, tn), lambda bb, m, n: (0, n))]
    out_specs = [pl.BlockSpec((tn // LANES, 1, dil, tm // dil, LANES), lambda bb, m, n: (n, bb, 0, m, 0))]
    out_shape = [jax.ShapeDtypeStruct((N // LANES, B, dil, L, LANES), BF16)]
    args = [xn, w, b]
    nbytes = 2 * (tm * D * 2 + D * tn * 2 + tm * tn * 2) + 2 * tm * tn * 4
    if rider is not None:
        w2d, cblk, fc = rider
        rows = w2d.shape[0]
        steps = B * mt * nn
        nblk = 1
        while nblk * 2 <= steps and rows % (nblk * 2) == 0:
            nblk *= 2
        rb = rows // nblk
        rmap = lambda bb, m, n: (jnp.minimum((bb * mt + m) * nn + n, nblk - 1), cblk)
        in_specs.append(pl.BlockSpec((rb, fc), rmap))
        out_specs.append(pl.BlockSpec((rb, fc), lambda bb, m, n: (rmap(bb, m, n)[0], 0)))
        out_shape.append(jax.ShapeDtypeStruct((rows, fc), BF16))
        args.append(w2d)
        nbytes += 2 * rb * fc * 6
    outs = pl.pallas_call(
        kern, name=f"proj_attn_dil{dil}",
        grid=(B, mt, nn),
        in_specs=in_specs,
        out_specs=out_specs,
        out_shape=out_shape,
        scratch_shapes=[pltpu.VMEM((tn // LANES, tm, LANES), F32)],
        compiler_params=pltpu.CompilerParams(
            dimension_semantics=("arbitrary", "arbitrary", "arbitrary"),
            vmem_limit_bytes=_vmem_limit(nbytes)),
    )(*args)
    return outs if rider is not None else outs[0]


def _conv_kernel(cur_ref, prev_ref, next_ref, w_ref, cb_ref, g_ref, b_ref, o_ref, ext_ref, y_ref, sh_ref, *, C):
    i = pl.program_id(1)
    last = pl.num_programs(1) - 1
    tp = cur_ref.shape[1]

    def glu(blk):
        a = blk[:, :C].astype(F32)
        gate = blk[:, C:].astype(F32)
        return a * (1.0 / (1.0 + jnp.exp(-gate)))

    ext_ref[0:CONV_HALO] = glu(prev_ref[0]) * jnp.where(i > 0, 1.0, 0.0)
    ext_ref[CONV_HALO:CONV_HALO + tp] = glu(cur_ref[0])
    ext_ref[CONV_HALO + tp:] = glu(next_ref[0]) * jnp.where(i < last, 1.0, 0.0)

    first = CONV_HALO - CONV_WIDTH // 2

    sub = 8
    span = sh_ref.shape[1]

    def chan_block(cb, carry):
        c0 = pl.multiple_of(cb * LANES, LANES)
        for b in range(1, sub):
            sh_ref[b] = ext_ref[pl.ds(b, span), pl.ds(c0, LANES)]
        acc = jnp.zeros((tp, LANES), F32)
        for d in range(CONV_WIDTH):
            q = first + d
            a8, b = (q // sub) * sub, q % sub
            if b == 0:
                win = ext_ref[pl.ds(a8, tp), pl.ds(c0, LANES)]
            else:
                win = sh_ref[b, pl.ds(a8, tp), :]
            acc = acc + win * w_ref[pl.ds(d, 1), pl.ds(c0, LANES)]
        y_ref[:, pl.ds(c0, LANES)] = acc + cb_ref[:, pl.ds(c0, LANES)]
        return carry

    lax.fori_loop(0, C // LANES, chan_block, 0)
    y = _ln(y_ref[...], g_ref[...], b_ref[...])
    o_ref[0] = (y * (1.0 / (1.0 + jnp.exp(-y)))).astype(o_ref.dtype)


def _conv_module(ag, conv_w, conv_b, ln_g, ln_b, B, S, tp):
    C = ag.shape[1] // 2
    ag3 = ag.reshape(B, S, 2 * C)
    hb = tp // CONV_HALO
    nh = S // CONV_HALO
    out = pl.pallas_call(
        functools.partial(_conv_kernel, C=C), name="conv_module",
        grid=(B, S // tp),
        in_specs=[pl.BlockSpec((1, tp, 2 * C), lambda b, i: (b, i, 0)),
                  pl.BlockSpec((1, CONV_HALO, 2 * C), lambda b, i: (b, jnp.maximum(i * hb - 1, 0), 0)),
                  pl.BlockSpec((1, CONV_HALO, 2 * C), lambda b, i: (b, jnp.minimum((i + 1) * hb, nh - 1), 0)),
                  pl.BlockSpec((CONV_WIDTH, C), lambda b, i: (0, 0)),
                  pl.BlockSpec((1, C), lambda b, i: (0, 0)),
                  pl.BlockSpec((1, C), lambda b, i: (0, 0)),
                  pl.BlockSpec((1, C), lambda b, i: (0, 0))],
        out_specs=pl.BlockSpec((1, tp, C), lambda b, i: (b, i, 0)),
        out_shape=jax.ShapeDtypeStruct((B, S, C), BF16),
        scratch_shapes=[pltpu.VMEM((tp + 2 * CONV_HALO, C), F32), pltpu.VMEM((tp, C), F32),
                        pltpu.VMEM((8, tp + 2 * CONV_HALO - 8, LANES), F32)],
        compiler_params=pltpu.CompilerParams(
            dimension_semantics=("parallel", "parallel"),
            vmem_limit_bytes=_vmem_limit(2 * tp * 2 * C * 2 + 2 * tp * C * 2 + 3 * tp * C * 4)),
    )(ag3, ag3, ag3, conv_w, conv_b, ln_g, ln_b)
    return out.reshape(B * S, C)


def _attn_kernel(slopes_ref, *refs, tp, dils, heads):
    ng = len(dils)
    ins = refs[:7 * ng]
    o_ref = refs[7 * ng]
    scr = refs[7 * ng + 1:]
    kwins, vwins = scr[:ng], scr[ng:2 * ng]
    o_nat, m_nat, l_nat = scr[2 * ng:2 * ng + 3]

    h = pl.program_id(1)
    i = pl.program_id(2)
    W = HALF_WINDOW

    for g, dil in enumerate(dils):
        q_ref, kc, kp, kn, vc, vp, vn = ins[7 * g:7 * g + 7]
        kwin, vwin = kwins[g], vwins[g]
        n = tp // dil
        L = n * pl.num_programs(2)
        sq = min(128, n)
        nsub = n // sq
        for win, cur, prev, nxt in ((kwin, kc, kp, kn), (vwin, vc, vp, vn)):
            win[:, 0:W] = prev[0, 0]
            win[:, W:W + n] = cur[0, 0]
            win[:, W + n:] = nxt[0, 0]

        slope = slopes_ref[g, h] * float(dil)
        row = lax.broadcasted_iota(jnp.int32, (sq, sq + 2 * W), 0)
        col = lax.broadcasted_iota(jnp.int32, (sq, sq + 2 * W), 1)
        dist = jnp.abs(col - W - row)
        bias = jnp.where(dist <= W, -slope * dist.astype(F32), NEG_INF)
        col1 = lax.broadcasted_iota(jnp.int32, (1, sq + 2 * W), 1)

        for r in range(dil):
            for sub in range(nsub):
                qs = q_ref[0, 0, r, sub * sq:(sub + 1) * sq, :]
                kw = kwin[r, sub * sq:sub * sq + sq + 2 * W, :]
                vw = vwin[r, sub * sq:sub * sq + sq + 2 * W, :]
                s = lax.dot_general(qs, kw, (((1,), (1,)), ((), ())), preferred_element_type=F32) + bias
                if sub == 0 or sub == nsub - 1:
                    kidx = i * n + (sub * sq - W) + col1
                    s = s + jnp.where((kidx >= 0) & (kidx < L), 0.0, NEG_INF)
                m = jnp.max(s, axis=-1, keepdims=True)
                p = jnp.exp(s - m)
                l = jnp.sum(p, axis=-1, keepdims=True)
                o = jnp.dot(p.astype(BF16), vw, preferred_element_type=F32)
                if dil == 1:
                    rows = pl.ds(sub * sq, sq)
                else:
                    rows = pl.ds(sub * sq * dil + r, sq, stride=dil)
                o_nat[g, rows, :] = o
                m_nat[g, rows, :] = jnp.broadcast_to(m, (sq, LANES))
                l_nat[g, rows, :] = jnp.broadcast_to(l, (sq, LANES))

    ch = min(128, tp)

    def merge(c, carry):
        r0 = pl.multiple_of(c * ch, ch)
        ms = [m_nat[g, pl.ds(r0, ch), :] for g in range(ng)]
        mx = functools.reduce(jnp.maximum, ms)
        num = jnp.zeros((ch, LANES), F32)
        den = jnp.zeros((ch, LANES), F32)
        for g in range(ng):
            wg = jnp.exp(ms[g] - mx)
            num = num + wg * o_nat[g, pl.ds(r0, ch), :]
            den = den + wg * l_nat[g, pl.ds(r0, ch), :]
        o_ref[0, pl.ds(r0, ch), :] = (num / den).astype(o_ref.dtype)
        return carry

    lax.fori_loop(0, tp // ch, merge, 0)


def _attention(qkvs, slopes, B, S, heads, tp):
    dils = tuple(d for _, d in ATT_GROUPS)
    W = HALF_WINDOW
    H = heads
    in_specs = [pl.BlockSpec(memory_space=pltpu.SMEM)]
    args = [slopes]
    scratch_k, scratch_v = [], []
    nbytes = 0
    for g, dil in enumerate(dils):
        n = tp // dil
        L = S // dil
        hb = n // W
        nh = L // W

        def cur_map(which):
            return lambda b, h, i: (which * H + h, b, 0, i, 0)

        def prev_map(which, hb=hb):
            return lambda b, h, i: (which * H + h, b, 0, jnp.maximum(i * hb - 1, 0), 0)

        def next_map(which, hb=hb, nh=nh):
            return lambda b, h, i: (which * H + h, b, 0, jnp.minimum((i + 1) * hb, nh - 1), 0)

        blk = (1, 1, dil, n, LANES)
        halo = (1, 1, dil, W, LANES)
        in_specs += [pl.BlockSpec(blk, cur_map(0)),
                     pl.BlockSpec(blk, cur_map(1)), pl.BlockSpec(halo, prev_map(1)), pl.BlockSpec(halo, next_map(1)),
                     pl.BlockSpec(blk, cur_map(2)), pl.BlockSpec(halo, prev_map(2)), pl.BlockSpec(halo, next_map(2))]
        args += [qkvs[g]] * 7
        scratch_k.append(pltpu.VMEM((dil, n + 2 * W, LANES), BF16))
        scratch_v.append(pltpu.VMEM((dil, n + 2 * W, LANES), BF16))
        nbytes += 2 * (3 * tp + 4 * dil * W) * LANES * 2 + 2 * (tp + 2 * W * dil) * LANES * 2
    nbytes += 9 * tp * LANES * 4 + 2 * tp * LANES * 2
    ng = len(dils)
    out = pl.pallas_call(
        functools.partial(_attn_kernel, tp=tp, dils=dils, heads=H), name="attention",
        grid=(B, H, S // tp),
        in_specs=in_specs,
        out_specs=pl.BlockSpec((1, tp, LANES), lambda b, h, i: (b, i, h)),
        out_shape=jax.ShapeDtypeStruct((B, S, H * LANES), BF16),
        scratch_shapes=scratch_k + scratch_v + [pltpu.VMEM((ng, tp, LANES), F32)] * 3,
        compiler_params=pltpu.CompilerParams(
            dimension_semantics=("parallel", "parallel", "parallel"),
            vmem_limit_bytes=_vmem_limit(nbytes)),
    )(*args)
    return out.reshape(B * S, H * LANES)


def _mix_proj_kernel(u_ref, o_ref, wu_ref, wo_ref, b_ref, y_ref):
    y_ref[...] = (jnp.dot(u_ref[...], wu_ref[...], preferred_element_type=F32)
                  + jnp.dot(o_ref[...], wo_ref[...], preferred_element_type=F32) + b_ref[...])


def _mix_proj(u, o, wu, wo, b, tm, tn):
    T, C = u.shape
    GW = o.shape[1]
    D = wu.shape[1]
    return pl.pallas_call(
        _mix_proj_kernel, name="mix_proj",
        grid=(T // tm, D // tn),
        in_specs=[pl.BlockSpec((tm, C), lambda m, n: (m, 0)),
                  pl.BlockSpec((tm, GW), lambda m, n: (m, 0)),
                  pl.BlockSpec((C, tn), lambda m, n: (0, n)),
                  pl.BlockSpec((GW, tn), lambda m, n: (0, n)),
                  pl.BlockSpec((1, tn), lambda m, n: (0, n))],
        out_specs=pl.BlockSpec((tm, tn), lambda m, n: (m, n)),
        out_shape=jax.ShapeDtypeStruct((T, D), F32),
        compiler_params=pltpu.CompilerParams(
            dimension_semantics=("parallel", "arbitrary"),
            vmem_limit_bytes=_vmem_limit(2 * (tm * (C + GW) * 2 + (C + GW) * tn * 2 + tm * tn * 4) + tm * tn * 4)),
    )(u, o, wu, wo, b)


def _ln1_router_kernel(*refs, alpha, tile_starts, n_groups, per):
    nk = len(tile_starts) - 1
    trunk_refs = [refs[2 * k:2 * k + 2] for k in range(nk)]
    (lig_ref, lib_ref, g1_ref, b1_ref, wh_ref, whl_ref, br_ref, x1_ref, gate_ref, idx_ref) = refs[2 * nk:]
    m = pl.program_id(0)
    tm = x1_ref.shape[0]

    for k, (x_ref, mix_ref) in enumerate(trunk_refs):
        @pl.when((m >= tile_starts[k]) & (m < tile_starts[k + 1]))
        def _():
            r = alpha * _ln(x_ref[...], lig_ref[...], lib_ref[...]) + mix_ref[...]
            x1_ref[...] = _ln(r, g1_ref[...], b1_ref[...])

    x = x1_ref[...]
    xh = x.astype(BF16)
    xl = (x - xh.astype(F32)).astype(BF16)
    rl = wh_ref.shape[1]
    both = jnp.dot(xh, whl_ref[...], preferred_element_type=F32)
    lg = both[:, :rl] + both[:, rl:] + jnp.dot(xl, wh_ref[...], preferred_element_type=F32) + br_ref[...]
    lane = lax.broadcasted_iota(jnp.int32, lg.shape, 1)
    big = jnp.int32(lg.shape[1])

    def top1(v):
        mx = jnp.max(v, axis=-1, keepdims=True)
        return mx, jnp.min(jnp.where(v == mx, lane, big), axis=-1, keepdims=True)

    gl = jnp.where(lane < n_groups, lg, NEG_INF)
    gmax, g_star = top1(gl)
    p_group = 1.0 / jnp.sum(jnp.exp(gl - gmax), axis=-1, keepdims=True)
    first = n_groups + g_star * per
    el = jnp.where((lane >= first) & (lane < first + per), lg, NEG_INF)
    v0, i0 = top1(el)
    v1, i1 = top1(jnp.where(lane == i0, NEG_INF, el))
    e1 = jnp.exp(v1 - v0)
    g0 = p_group / (1.0 + e1)
    g1 = g0 * e1
    gate_ref[...] = jnp.where(lane == 0, g0, jnp.where(lane == 1, g1, 0.0))
    idx_ref[...] = jnp.where(lane == 0, i0 - n_groups, jnp.where(lane == 1, i1 - n_groups, 0))


def _ln1_router(trunk_inputs, ln_in_g, ln_in_b, ln1_g, ln1_b, wrh, wrl, br, alpha, n_groups, per, tm):
    D = trunk_inputs[0][0].shape[1]
    RL = wrh.shape[1]
    tile_starts = [0]
    for x, _ in trunk_inputs:
        tile_starts.append(tile_starts[-1] + x.shape[0] // tm)
    nm = tile_starts[-1]
    const = lambda m: (0, 0)
    in_specs, args = [], []
    for k, (x, mix) in enumerate(trunk_inputs):
        def rows(m, lo=tile_starts[k], cnt=tile_starts[k + 1] - tile_starts[k]):
            return (jnp.clip(m - lo, 0, cnt - 1), 0)
        in_specs += [pl.BlockSpec((tm, D), rows), pl.BlockSpec((tm, D), rows)]
        args += [x, mix]
    in_specs += [pl.BlockSpec((1, D), const)] * 4 + [pl.BlockSpec((D, RL), const), pl.BlockSpec((D, 2 * RL), const),
                                                      pl.BlockSpec((1, RL), const)]
    args += [ln_in_g, ln_in_b, ln1_g, ln1_b, wrh, jnp.concatenate([wrh, wrl], axis=1), br]
    nk = len(trunk_inputs)
    return pl.pallas_call(
        functools.partial(_ln1_router_kernel, alpha=alpha, tile_starts=tuple(tile_starts), n_groups=n_groups,
                          per=per), name="ln1_router",
        grid=(nm,),
        in_specs=in_specs,
        out_specs=[pl.BlockSpec((tm, D), lambda m: (m, 0)),
                   pl.BlockSpec((tm, RL), lambda m: (m, 0)), pl.BlockSpec((tm, RL), lambda m: (m, 0))],
        out_shape=[jax.ShapeDtypeStruct((nm * tm, D), F32),
                   jax.ShapeDtypeStruct((nm * tm, RL), F32), jax.ShapeDtypeStruct((nm * tm, RL), jnp.int32)],
        compiler_params=pltpu.CompilerParams(
            dimension_semantics=("parallel",),
            vmem_limit_bytes=_vmem_limit(2 * (2 * nk + 1) * tm * D * 4 + 2 * tm * D * 2 + 6 * D * RL * 2)),
    )(*args)


def _start_row_gather(src_hbm, idx_ref, dst, sem, n_rows):
    for r in range(n_rows):
        pltpu.make_async_copy(src_hbm.at[pl.ds(idx_ref[0, 0, r], 1)], dst.at[pl.ds(r, 1)], sem).start()


def _wait_row_gather(src_hbm, dst, sem, n_rows):
    for r in range(n_rows):
        pltpu.make_async_copy(src_hbm.at[pl.ds(0, 1)], dst.at[pl.ds(r, 1)], sem).wait()


def _moe_up_kernel(te_ref, tv_ref, idx_ref, idxn_ref, x_hbm, *rest, n_pieces):
    wg_refs, wu_refs = rest[:n_pieces], rest[n_pieces:2 * n_pieces]
    h_ref, xbuf, sem = rest[2 * n_pieces:]
    t = pl.program_id(0)
    nt = pl.num_programs(0)
    tm = xbuf.shape[1]
    slot = t % 2

    @pl.when((t == 0) & (tv_ref[0] > 0))
    def _():
        _start_row_gather(x_hbm, idx_ref, xbuf.at[0], sem.at[0], tm)

    nxt = jnp.minimum(t + 1, nt - 1)

    @pl.when((t + 1 < nt) & (tv_ref[nxt] > 0))
    def _():
        _start_row_gather(x_hbm, idxn_ref, xbuf.at[1 - slot], sem.at[1 - slot], tm)

    @pl.when(tv_ref[t] > 0)
    def _():
        _wait_row_gather(x_hbm, xbuf.at[slot], sem.at[slot], tm)
        xb = xbuf[slot].astype(BF16)
        fc = h_ref.shape[1] // n_pieces
        for p in range(n_pieces):
            a = jnp.dot(xb, wg_refs[p][0], preferred_element_type=F32)
            b = jnp.dot(xb, wu_refs[p][0], preferred_element_type=F32)
            h_ref[:, p * fc:(p + 1) * fc] = (a * (1.0 / (1.0 + jnp.exp(-a))) * b).astype(h_ref.dtype)

    @pl.when(tv_ref[t] == 0)
    def _():
        h_ref[...] = jnp.zeros_like(h_ref)


def _moe_up(tile_e, tile_valid, buf_tok3, x1, wg_pieces, wu_pieces, tm):
    nt = tile_e.shape[0]
    n_pieces = len(wg_pieces)
    E, D, fc = wg_pieces[0].shape
    F = fc * n_pieces
    wspec = pl.BlockSpec((1, D, fc), lambda t, te, tv: (te[t], 0, 0))
    grid_spec = pltpu.PrefetchScalarGridSpec(
        num_scalar_prefetch=2,
        grid=(nt,),
        in_specs=[pl.BlockSpec((1, 1, tm), lambda t, te, tv: (t, 0, 0), memory_space=pltpu.SMEM),
                  pl.BlockSpec((1, 1, tm), lambda t, te, tv: (jnp.minimum(t + 1, nt - 1), 0, 0),
                               memory_space=pltpu.SMEM),
                  pl.BlockSpec(memory_space=pl.ANY)] + [wspec] * (2 * n_pieces),
        out_specs=pl.BlockSpec((tm, F), lambda t, te, tv: (t, 0)),
        scratch_shapes=[pltpu.VMEM((2, tm, D), F32), pltpu.SemaphoreType.DMA((2,))])
    return pl.pallas_call(
        functools.partial(_moe_up_kernel, n_pieces=n_pieces), name="moe_up",
        grid_spec=grid_spec,
        out_shape=jax.ShapeDtypeStruct((nt * tm, F), BF16),
        compiler_params=pltpu.CompilerParams(
            dimension_semantics=("arbitrary",),
            vmem_limit_bytes=_vmem_limit(4 * D * F * 2 + 2 * tm * D * 4 + 2 * tm * F * 2 + tm * D * 2)),
    )(tile_e, tile_valid, buf_tok3, buf_tok3, x1, *wg_pieces, *wu_pieces)


def _bf16_bits(v):
    u = pltpu.bitcast(v, jnp.uint32)
    return u + jnp.uint32(0x7FFF) + ((u >> 16) & jnp.uint32(1))


def _moe_down_kernel(te_ref, tv_ref, h_ref, wlo_ref, whi_ref, y_ref):
    t = pl.program_id(0)
    valid = tv_ref[t] > 0

    @pl.when(valid)
    def _():
        h = h_ref[...]
        lo = _bf16_bits(jnp.dot(h, wlo_ref[0], preferred_element_type=F32)) >> 16
        hi = _bf16_bits(jnp.dot(h, whi_ref[0], preferred_element_type=F32)) & jnp.uint32(0xFFFF0000)
        y_ref[...] = lo | hi

    @pl.when(jnp.logical_not(valid))
    def _():
        y_ref[...] = jnp.zeros_like(y_ref)


def _moe_down(tile_e, tile_valid, hmid, wd_lo, wd_hi, tm):
    nt = tile_e.shape[0]
    E, F, half = wd_lo.shape
    wspec = pl.BlockSpec((1, F, half), lambda t, te, tv: (te[t], 0, 0))
    grid_spec = pltpu.PrefetchScalarGridSpec(
        num_scalar_prefetch=2,
        grid=(nt,),
        in_specs=[pl.BlockSpec((tm, F), lambda t, te, tv: (t, 0)), wspec, wspec],
        out_specs=pl.BlockSpec((tm, half), lambda t, te, tv: (t, 0)))
    return pl.pallas_call(
        _moe_down_kernel, name="moe_down",
        grid_spec=grid_spec,
        out_shape=jax.ShapeDtypeStruct((nt * tm, half), jnp.uint32),
        compiler_params=pltpu.CompilerParams(
            dimension_semantics=("arbitrary",),
            vmem_limit_bytes=_vmem_limit(2 * (2 * F * half * 2 + tm * F * 2 + tm * half * 4) + 2 * tm * half * 4)),
    )(tile_e, tile_valid, hmid, wd_lo, wd_hi)


def _combine_kernel(idx_ref, idxn_ref, x1_ref, gate_ref, g_ref, b_ref, ys_hbm, o_ref, ybuf, sem, *, alpha):
    t = pl.program_id(0)
    nt = pl.num_programs(0)
    tm, D = x1_ref.shape
    half = D // 2
    slot = t % 2

    @pl.when(t == 0)
    def _():
        _start_row_gather(ys_hbm, idx_ref, ybuf.at[0], sem.at[0], TOP_K * tm)

    @pl.when(t + 1 < nt)
    def _():
        _start_row_gather(ys_hbm, idxn_ref, ybuf.at[1 - slot], sem.at[1 - slot], TOP_K * tm)

    _wait_row_gather(ys_hbm, ybuf.at[slot], sem.at[slot], TOP_K * tm)
    gates = gate_ref[...]
    w0 = ybuf[slot, 0:tm, :]
    w1 = ybuf[slot, tm:2 * tm, :]
    unpack_lo = lambda w: pltpu.bitcast(w << 16, F32)
    unpack_hi = lambda w: pltpu.bitcast(w & jnp.uint32(0xFFFF0000), F32)
    g0, g1 = gates[:, 0:1], gates[:, 1:2]
    r_lo = alpha * x1_ref[:, :half] + g0 * unpack_lo(w0) + g1 * unpack_lo(w1)
    r_hi = alpha * x1_ref[:, half:] + g0 * unpack_hi(w0) + g1 * unpack_hi(w1)
    mu = (jnp.sum(r_lo, axis=-1, keepdims=True) + jnp.sum(r_hi, axis=-1, keepdims=True)) / D
    c_lo = r_lo - mu
    c_hi = r_hi - mu
    var = (jnp.sum(c_lo * c_lo, axis=-1, keepdims=True) + jnp.sum(c_hi * c_hi, axis=-1, keepdims=True)) / D
    rstd = lax.rsqrt(var + LN_EPS)
    o_ref[:, :half] = c_lo * rstd * g_ref[:, :half] + b_ref[:, :half]
    o_ref[:, half:] = c_hi * rstd * g_ref[:, half:] + b_ref[:, half:]


def _combine(dest3, x1, row_offset, T, gates, ln_g, ln_b, ys, alpha, tm):
    D = x1.shape[1]
    nt = T // tm
    mo = row_offset // tm
    return pl.pallas_call(
        functools.partial(_combine_kernel, alpha=alpha), name="moe_combine",
        grid=(nt,),
        in_specs=[pl.BlockSpec((1, 1, TOP_K * tm), lambda t: (mo + t, 0, 0), memory_space=pltpu.SMEM),
                  pl.BlockSpec((1, 1, TOP_K * tm), lambda t: (mo + jnp.minimum(t + 1, nt - 1), 0, 0),
                               memory_space=pltpu.SMEM),
                  pl.BlockSpec((tm, D), lambda t: (mo + t, 0)),
                  pl.BlockSpec((tm, TOP_K), lambda t: (mo + t, 0)),
                  pl.BlockSpec((1, D), lambda t: (0, 0)),
                  pl.BlockSpec((1, D), lambda t: (0, 0)),
                  pl.BlockSpec(memory_space=pl.ANY)],
        out_specs=pl.BlockSpec((tm, D), lambda t: (t, 0)),
        out_shape=jax.ShapeDtypeStruct((T, D), F32),
        scratch_shapes=[pltpu.VMEM((2, TOP_K * tm, D // 2), jnp.uint32), pltpu.SemaphoreType.DMA((2,))],
        compiler_params=pltpu.CompilerParams(
            dimension_semantics=("arbitrary",),
            vmem_limit_bytes=_vmem_limit(4 * tm * D * 4 + 2 * TOP_K * tm * D * 2 + 4 * tm * D * 4)),
    )(dest3, dest3, x1, gates, ln_g, ln_b, ys)


def _dispatch_plan(flat_e, n_experts, tile):
    A = flat_e.shape[0]
    onehot = (flat_e[:, None] == jnp.arange(n_experts, dtype=jnp.int32)[None, :])
    blk = 256
    oh3 = onehot.astype(BF16).reshape(A // blk, blk, n_experts)
    tri = (jnp.arange(blk)[:, None] > jnp.arange(blk)[None, :]).astype(BF16)
    local = jnp.einsum('ij,bjk->bik', tri, oh3, preferred_element_type=F32)
    blk_tot = jnp.sum(oh3.astype(F32), axis=1)
    blk_off = jnp.cumsum(blk_tot, axis=0) - blk_tot
    excl = (local + blk_off[:, None, :]).reshape(A, n_experts)
    counts = jnp.sum(blk_tot, axis=0).astype(jnp.int32)
    rank = jnp.sum(jnp.where(onehot, excl, 0.0), axis=1).astype(jnp.int32)
    padded = (counts + tile - 1) // tile * tile
    pend = jnp.cumsum(padded)
    pstart = pend - padded
    dest = (pstart[flat_e] + rank).astype(jnp.int32)
    n_tiles = A // tile + n_experts
    tile_start = jnp.arange(n_tiles, dtype=jnp.int32) * tile
    tile_e = jnp.minimum(jnp.sum((pend[None, :] <= tile_start[:, None]).astype(jnp.int32), axis=1), n_experts - 1)
    tile_valid = (tile_start < pend[-1]).astype(jnp.int32)
    order = jnp.argsort(flat_e, stable=True).astype(jnp.int32)
    start = jnp.cumsum(counts) - counts
    rows = jnp.arange(n_tiles * tile, dtype=jnp.int32)
    row_e = jnp.repeat(tile_e, tile)
    r = rows - pstart[row_e]
    buf_tok = jnp.where(r < counts[row_e], order[jnp.clip(start[row_e] + r, 0, A - 1)] // TOP_K, 0)
    return dest, buf_tok, tile_e, tile_valid


def _tiles(D, S, C, GW, F):
    max_dil = max(d for _, d in ATT_GROUPS)
    t = {}
    t['ln_rows'] = min(256, S)
    t['proj_rows'] = min(1024, S)
    t['proj_cols'] = min(512, GW)
    t['conv_rows'] = min(256, S)
    t['attn_rows'] = min(2048, S)
    t['moe_rows'] = min(256, S)
    t['router_rows'] = min(256, S)
    assert t['proj_rows'] % (16 * max_dil) == 0 and t['attn_rows'] % (HALF_WINDOW * max_dil) == 0
    assert S % t['proj_rows'] == 0 and S % t['attn_rows'] == 0 and S % t['conv_rows'] == 0
    return t


def kernel(x_prompt, x_sample, ln_in_g, ln_in_b, w_in, b_in, conv_w, conv_b, conv_ln_g, conv_ln_b, w_out, b_out,
           ln1_g, ln1_b, w_router_group, b_router_group, w_router_expert, b_router_expert, w_gate, w_up, w_down,
           ln2_g, ln2_b):
    depth = w_in.shape[0]
    assert depth == 1
    assert all(w // (2 * d) == HALF_WINDOW for w, d in ATT_GROUPS)
    alpha = (2.0 * depth) ** 0.25
    D = x_prompt.shape[-1]
    C = conv_w.shape[-1]
    n_att = len(ATT_GROUPS)
    GW = (w_in.shape[-1] - 2 * C) // (3 * n_att)
    H = GW // HEAD_DIM
    n_groups = w_router_group.shape[-1]
    n_experts = w_router_expert.shape[-1]
    F = w_gate.shape[-1]
    row = lambda v: v.reshape(1, -1).astype(F32)

    w_in_b = w_in[0].astype(BF16)
    b_in_r = row(b_in[0])
    w_sections = [(w_in_b[:, :2 * C], b_in_r[:, :2 * C])]
    off = 2 * C
    for _ in range(n_att):
        w_sections.append((w_in_b[:, off:off + 3 * GW], b_in_r[:, off:off + 3 * GW]))
        off += 3 * GW
    w_out_b = w_out[0].astype(BF16)
    wu_out, wo_out = w_out_b[:C], w_out_b[C:]
    RL = LANES
    w_r = jnp.concatenate([w_router_group[0], w_router_expert[0]], axis=1).astype(F32)
    w_r = jnp.pad(w_r, ((0, 0), (0, RL - w_r.shape[1])))
    wrh = w_r.astype(BF16)
    wrl = (w_r - wrh.astype(F32)).astype(BF16)
    b_r = jnp.pad(jnp.concatenate([b_router_group[0], b_router_expert[0]]).astype(F32),
                  (0, RL - n_groups - n_experts)).reshape(1, RL)
    n_heads = n_att * H
    slopes = jnp.exp2(-8.0 * jnp.arange(1, n_heads + 1, dtype=F32) / n_heads).reshape(n_att, H)

    n_fp = 2 if F % (2 * LANES) == 0 else 1
    fc = F // n_fp
    riders = [(wmat[0].reshape(n_experts * D, F), c, fc, (name, c))
              for c in range(n_fp) for name, wmat in (('gate', w_gate), ('up', w_up))]
    n_rider_hosts = (len(riders) + 1) // 2
    down_riders = [(w_down[0].reshape(n_experts * F, D), c, D // 2, ('down', c)) for c in range(2)]
    expert_bf16 = {}
    trunks = [x_prompt, x_sample]
    total = sum(x.shape[0] * x.shape[1] for x in trunks)
    staged = []
    offsets = []
    row_offset = 0
    for x in trunks:
        B, S, _ = x.shape
        T = B * S
        t = _tiles(D, S, C, GW, F)
        x2d = x.reshape(T, D)
        xn = _ln_cast(x2d, row(ln_in_g), row(ln_in_b), t['ln_rows'])
        ag = _proj_nat(xn, w_sections[0][0], w_sections[0][1], t['proj_rows'], min(t['proj_cols'], 2 * C))
        u = _conv_module(ag, conv_w[0].astype(F32), row(conv_b[0]), row(conv_ln_g[0]), row(conv_ln_b[0]),
                         B, S, t['conv_rows'])
        qkvs = []
        for g, (_, dil) in enumerate(ATT_GROUPS):
            rider = None
            if riders and g < n_rider_hosts:
                rider = riders.pop(0)
            elif down_riders and g == n_att - 1:
                rider = down_riders.pop(0)
            res = _proj_heads(xn, w_sections[1 + g][0], w_sections[1 + g][1], B, S, dil, GW,
                              t['proj_rows'], t['proj_cols'], rider=None if rider is None else rider[:3])
            if rider is not None:
                expert_bf16[rider[3]] = res[1].reshape(n_experts, -1, rider[2])
                res = res[0]
            qkvs.append(res)
        o = _attention(qkvs, slopes, B, S, H, t['attn_rows'])
        mix = _mix_proj(u, o, wu_out, wo_out, row(b_out[0]), t['proj_rows'], t['proj_cols'])
        staged.append((x2d, mix))
        offsets.append((row_offset, T, t))
        row_offset += T
    t0 = offsets[0][2]
    x1, gate_l, idx_l = _ln1_router(staged, row(ln_in_g), row(ln_in_b), row(ln1_g[0]), row(ln1_b[0]),
                                    wrh, wrl, b_r, alpha, n_groups, n_experts // n_groups, t0['router_rows'])
    gates = gate_l[:, :TOP_K]

    tm = t0['moe_rows']
    dest, buf_tok, tile_e, tile_valid = _dispatch_plan(idx_l[:, :TOP_K].reshape(total * TOP_K), n_experts, tm)
    nt = tile_e.shape[0]
    hmid = _moe_up(tile_e, tile_valid, buf_tok.reshape(nt, 1, tm), x1,
                   [expert_bf16[('gate', c)] for c in range(n_fp)],
                   [expert_bf16[('up', c)] for c in range(n_fp)], tm)
    ys = _moe_down(tile_e, tile_valid, hmid, expert_bf16[('down', 0)], expert_bf16[('down', 1)], tm)
    dest3 = dest.reshape(total // tm, tm, TOP_K).transpose(0, 2, 1).reshape(total // tm, 1, TOP_K * tm)
    outs = []
    for x, (ro, T, t) in zip(trunks, offsets):
        y = _combine(dest3, x1, ro, T, gates, row(ln2_g[0]), row(ln2_b[0]), ys, alpha, tm)
        outs.append(y.reshape(x.shape))
    return tuple(outs)
```
